```python
import math
import jax, jax.numpy as jnp
from jax import lax
import numpy as np

D_MODEL = 4096
BATCH = 2
SEQ = 8192
DEPTH = 2

LRU_WIDTH = D_MODEL
LRU_BLOCKS = 16
LRU_BLOCK_W = LRU_WIDTH // LRU_BLOCKS
CONV_WIDTH = 4
LRU_C = 8.0
N_Q_HEADS = 32
N_KV_HEADS = 8
HEAD_DIM = D_MODEL // N_Q_HEADS
GROUP = N_Q_HEADS // N_KV_HEADS
WINDOW = 128
BLOCK = 128
ROPE_THETA = 10000.0
D_FF = 7 * D_MODEL // 2
N_EXPERTS = 8
TOP_K = 2
D_EXPERT = D_MODEL
LN_EPS = 1e-5
DEEPNORM_ALPHA = (2 * DEPTH) ** 0.25
DEEPNORM_BETA = (8 * DEPTH) ** -0.25

kernel_name = 'hybrid_rglru_swa_sink_moe_deepnorm'


def layer_norm(x, g, b):
    xf = x.astype(jnp.float32)
    mu = jnp.mean(xf, axis=-1, keepdims=True)
    var = jnp.mean(jnp.square(xf - mu), axis=-1, keepdims=True)
    y = (xf - mu) * lax.rsqrt(var + LN_EPS)
    return (y * g.astype(jnp.float32) + b.astype(jnp.float32)).astype(x.dtype)


def deepnorm_residual(x, y, g, b):
    return layer_norm(DEEPNORM_ALPHA * x + y, g, b)


def causal_depthwise_conv(u, w, b):
    y = lax.conv_general_dilated(
        u, w[:, None, :].astype(u.dtype), window_strides=(1,),
        padding=[(CONV_WIDTH - 1, 0)], dimension_numbers=('NWC', 'WIO', 'NWC'),
        feature_group_count=u.shape[-1])
    return y + b


def block_diag_linear(u, w, b):
    ub = u.reshape(u.shape[0], u.shape[1], LRU_BLOCKS, LRU_BLOCK_W)
    y = jnp.einsum('bshi,hij->bshj', ub, w) + b
    return y.reshape(u.shape)


def rg_lru(u, w_r, b_r, w_i, b_i, lam):
    r = jax.nn.sigmoid(block_diag_linear(u, w_r, b_r).astype(jnp.float32))
    i = jax.nn.sigmoid(block_diag_linear(u, w_i, b_i).astype(jnp.float32))
    log_a = -LRU_C * r * jax.nn.softplus(-lam.astype(jnp.float32))
    a = jnp.exp(log_a)
    bterm = jnp.sqrt(-jnp.expm1(2.0 * log_a)) * (i * u.astype(jnp.float32))

    def combine(c1, c2):
        a1, b1 = c1
        a2, b2 = c2
        return a1 * a2, a2 * b1 + b2

    _, h = lax.associative_scan(combine, (a, bterm), axis=1)
    return h.astype(u.dtype)


def recurrent_mixer(x, w_in, conv_w, conv_b, w_r, b_r, w_i, b_i, lam, w_out):
    proj = x @ w_in
    u, gate = jnp.split(proj, 2, axis=-1)
    u = causal_depthwise_conv(u, conv_w, conv_b)
    h = rg_lru(u, w_r, b_r, w_i, b_i, lam)
    return (h * jax.nn.gelu(gate, approximate=True)) @ w_out


def rope_tables(seq):
    pos = jnp.arange(seq, dtype=jnp.float32)
    inv = ROPE_THETA ** (-jnp.arange(0, HEAD_DIM, 2, dtype=jnp.float32) / HEAD_DIM)
    ang = pos[:, None] * inv[None, :]
    return jnp.cos(ang)[:, None, :], jnp.sin(ang)[:, None, :]


def apply_rope(t, cos, sin):
    half = HEAD_DIM // 2
    t1 = t[..., :half].astype(jnp.float32)
    t2 = t[..., half:].astype(jnp.float32)
    out = jnp.concatenate([t1 * cos - t2 * sin, t2 * cos + t1 * sin], axis=-1)
    return out.astype(t.dtype)


def sliding_window_attention(x, w_qkv, b_qkv, sinks, w_o):
    B, S, _ = x.shape
    qkv = x @ w_qkv + b_qkv
    q, k, v = jnp.split(qkv, [N_Q_HEADS * HEAD_DIM, (N_Q_HEADS + N_KV_HEADS) * HEAD_DIM], axis=-1)
    q = q.reshape(B, S, N_Q_HEADS, HEAD_DIM)
    k = k.reshape(B, S, N_KV_HEADS, HEAD_DIM)
    v = v.reshape(B, S, N_KV_HEADS, HEAD_DIM)
    cos, sin = rope_tables(S)
    q = apply_rope(q, cos, sin)
    k = apply_rope(k, cos, sin)

    nb = S // BLOCK
    qb = q.reshape(B, nb, BLOCK, N_KV_HEADS, GROUP, HEAD_DIM)
    kb = k.reshape(B, nb, BLOCK, N_KV_HEADS, HEAD_DIM)
    vb = v.reshape(B, nb, BLOCK, N_KV_HEADS, HEAD_DIM)

    def with_prev(t):
        prev = jnp.pad(t[:, :-1], ((0, 0), (1, 0), (0, 0), (0, 0), (0, 0)))
        return jnp.concatenate([prev, t], axis=2)

    kw = with_prev(kb)
    vw = with_prev(vb)
    s = jnp.einsum('bnqhgd,bnkhd->bnhgqk', qb, kw).astype(jnp.float32) * (HEAD_DIM ** -0.5)

    qi = jnp.arange(BLOCK)[:, None]
    kj = jnp.arange(2 * BLOCK)[None, :]
    delta = qi + BLOCK - kj
    band = (delta >= 0) & (delta < WINDOW)
    blk = jnp.arange(nb)[:, None, None]
    valid = band[None] & ((blk > 0) | (kj[None] >= BLOCK))
    s = jnp.where(valid[None, :, None, None], s, -jnp.inf)

    sink = sinks.astype(jnp.float32).reshape(N_KV_HEADS, GROUP)[None, None, :, :, None, None]
    m = jnp.maximum(jnp.max(s, axis=-1, keepdims=True), sink)
    p = jnp.exp(s - m)
    denom = jnp.sum(p, axis=-1, keepdims=True) + jnp.exp(sink - m)
    o = jnp.einsum('bnhgqk,bnkhd->bnqhgd', (p / denom).astype(vw.dtype), vw)
    return o.reshape(B, S, N_Q_HEADS * HEAD_DIM) @ w_o


def swiglu(x, w_gate_up, w_down):
    g, u = jnp.split(x @ w_gate_up, 2, axis=-1)
    return (jax.nn.silu(g) * u) @ w_down


def moe_swiglu(x, w_router, b_router, we_gate_up, we_down):
    B, S, D = x.shape
    t = x.reshape(B * S, D)
    logits = (t @ w_router).astype(jnp.float32) + b_router.astype(jnp.float32)
    top_val, top_idx = lax.top_k(logits, TOP_K)
    top_w = jax.nn.softmax(top_val, axis=-1)
    combine = jnp.sum(jax.nn.one_hot(top_idx, N_EXPERTS, dtype=jnp.float32) * top_w[..., None], axis=1)
    out = jnp.zeros_like(t)
    for e in range(N_EXPERTS):
        out = out + combine[:, e:e + 1].astype(t.dtype) * swiglu(t, we_gate_up[e], we_down[e])
    return out.reshape(B, S, D)


def _normal(key, shape, fan_in, scale=1.0):
    return jax.random.normal(key, shape, jnp.float32) * (fan_in ** -0.5) * scale


def setup_inputs(seed: int = 0) -> dict:
    key = jax.random.key(seed)
    ks = jax.random.split(key, 32)
    x = jax.random.normal(ks[0], (BATCH, SEQ, D_MODEL), jnp.float32)

    def gain(k):
        return 1.0 + 0.02 * jax.random.normal(k, (D_MODEL,), jnp.float32)

    def bias(k, shape):
        return 0.02 * jax.random.normal(k, shape, jnp.float32)

    l0_w_in = _normal(ks[1], (D_MODEL, 2 * LRU_WIDTH), D_MODEL)
    l0_conv_w = _normal(ks[2], (CONV_WIDTH, LRU_WIDTH), CONV_WIDTH)
    l0_conv_b = bias(ks[3], (LRU_WIDTH,))
    l0_w_rgate = _normal(ks[4], (LRU_BLOCKS, LRU_BLOCK_W, LRU_BLOCK_W), LRU_BLOCK_W)
    l0_b_rgate = bias(ks[5], (LRU_BLOCKS, LRU_BLOCK_W))
    l0_w_igate = _normal(ks[6], (LRU_BLOCKS, LRU_BLOCK_W, LRU_BLOCK_W), LRU_BLOCK_W)
    l0_b_igate = bias(ks[7], (LRU_BLOCKS, LRU_BLOCK_W))
    a_c = jax.random.uniform(ks[8], (LRU_WIDTH,), jnp.float32, 0.9, 0.999)
    a0 = a_c ** (1.0 / LRU_C)
    l0_lru_lambda = jnp.log(a0) - jnp.log1p(-a0)
    l0_w_out = _normal(ks[9], (LRU_WIDTH, D_MODEL), LRU_WIDTH, DEEPNORM_BETA)
    l0_ln1_g = gain(ks[10])
    l0_ln1_b = bias(ks[11], (D_MODEL,))
    l0_w_gate_up = _normal(ks[12], (D_MODEL, 2 * D_FF), D_MODEL)
    l0_w_down = _normal(ks[13], (D_FF, D_MODEL), D_FF, DEEPNORM_BETA)
    l0_ln2_g = gain(ks[14])
    l0_ln2_b = bias(ks[15], (D_MODEL,))

    w_q = _normal(ks[16], (D_MODEL, N_Q_HEADS * HEAD_DIM), D_MODEL)
    w_k = _normal(ks[17], (D_MODEL, N_KV_HEADS * HEAD_DIM), D_MODEL)
    w_v = _normal(ks[18], (D_MODEL, N_KV_HEADS * HEAD_DIM), D_MODEL, DEEPNORM_BETA)
    l1_w_qkv = jnp.concatenate([w_q, w_k, w_v], axis=1)
    l1_b_qkv = bias(ks[19], ((N_Q_HEADS + 2 * N_KV_HEADS) * HEAD_DIM,))
    l1_sinks = 0.5 * jax.random.normal(ks[20], (N_Q_HEADS,), jnp.float32)
    l1_w_o = _normal(ks[21], (N_Q_HEADS * HEAD_DIM, D_MODEL), N_Q_HEADS * HEAD_DIM, DEEPNORM_BETA)
    l1_ln1_g = gain(ks[22])
    l1_ln1_b = bias(ks[23], (D_MODEL,))
    l1_w_router = _normal(ks[24], (D_MODEL, N_EXPERTS), D_MODEL)
    l1_b_router = 0.01 * jax.random.normal(ks[25], (N_EXPERTS,), jnp.float32)
    l1_we_gate_up = _normal(ks[26], (N_EXPERTS, D_MODEL, 2 * D_EXPERT), D_MODEL)
    l1_we_down = _normal(ks[27], (N_EXPERTS, D_EXPERT, D_MODEL), D_EXPERT, DEEPNORM_BETA)
    l1_ln2_g = gain(ks[28])
    l1_ln2_b = bias(ks[29], (D_MODEL,))

    return {
        'x': x,
        'l0_w_in': l0_w_in, 'l0_conv_w': l0_conv_w, 'l0_conv_b': l0_conv_b,
        'l0_w_rgate': l0_w_rgate, 'l0_b_rgate': l0_b_rgate,
        'l0_w_igate': l0_w_igate, 'l0_b_igate': l0_b_igate,
        'l0_lru_lambda': l0_lru_lambda, 'l0_w_out': l0_w_out,
        'l0_ln1_g': l0_ln1_g, 'l0_ln1_b': l0_ln1_b,
        'l0_w_gate_up': l0_w_gate_up, 'l0_w_down': l0_w_down,
        'l0_ln2_g': l0_ln2_g, 'l0_ln2_b': l0_ln2_b,
        'l1_w_qkv': l1_w_qkv, 'l1_b_qkv': l1_b_qkv, 'l1_sinks': l1_sinks, 'l1_w_o': l1_w_o,
        'l1_ln1_g': l1_ln1_g, 'l1_ln1_b': l1_ln1_b,
        'l1_w_router': l1_w_router, 'l1_b_router': l1_b_router,
        'l1_we_gate_up': l1_we_gate_up, 'l1_we_down': l1_we_down,
        'l1_ln2_g': l1_ln2_g, 'l1_ln2_b': l1_ln2_b,
    }


def reference(x,
              l0_w_in, l0_conv_w, l0_conv_b, l0_w_rgate, l0_b_rgate, l0_w_igate, l0_b_igate,
              l0_lru_lambda, l0_w_out, l0_ln1_g, l0_ln1_b, l0_w_gate_up, l0_w_down,
              l0_ln2_g, l0_ln2_b,
              l1_w_qkv, l1_b_qkv, l1_sinks, l1_w_o, l1_ln1_g, l1_ln1_b,
              l1_w_router, l1_b_router, l1_we_gate_up, l1_we_down, l1_ln2_g, l1_ln2_b):
    layers = [
        (recurrent_mixer,
         (l0_w_in, l0_conv_w, l0_conv_b, l0_w_rgate, l0_b_rgate, l0_w_igate, l0_b_igate,
          l0_lru_lambda, l0_w_out),
         l0_ln1_g, l0_ln1_b,
         swiglu, (l0_w_gate_up, l0_w_down), l0_ln2_g, l0_ln2_b),
        (sliding_window_attention,
         (l1_w_qkv, l1_b_qkv, l1_sinks, l1_w_o),
         l1_ln1_g, l1_ln1_b,
         moe_swiglu, (l1_w_router, l1_b_router, l1_we_gate_up, l1_we_down), l1_ln2_g, l1_ln2_b),
    ]
    for i in range(DEPTH):
        mixer, mixer_params, g1, b1, ffn, ffn_params, g2, b2 = layers[i]
        x = deepnorm_residual(x, mixer(x, *mixer_params), g1, b1)
        x = deepnorm_residual(x, ffn(x, *ffn_params), g2, b2)
    return x
```

```python
import functools

import jax
import jax.numpy as jnp
from jax import lax
from jax.experimental import pallas as pl
from jax.experimental.pallas import tpu as pltpu

F32 = jnp.float32
BF16 = jnp.bfloat16

HEAD_DIM = 128
WINDOW = 128
CONV_WIDTH = 4
LRU_C = 8.0
TOP_K = 2
ROPE_THETA = 10000.0
LN_EPS = 1e-5
LANES = 128
VMEM_LIMIT_BYTES = 56 * 1024 * 1024


def _params(*sem):
    return pltpu.CompilerParams(dimension_semantics=sem, vmem_limit_bytes=VMEM_LIMIT_BYTES)


def _layer_norm_rows(y_ref, o_ref, g_ref, b_ref, rows, chunk=16):
    g = g_ref[...]
    b = b_ref[...]

    def body(c, carry):
        r0 = pl.multiple_of(c * chunk, chunk)
        y = y_ref[pl.ds(r0, chunk), :]
        mu = jnp.mean(y, axis=-1, keepdims=True)
        d = y - mu
        var = jnp.mean(d * d, axis=-1, keepdims=True)
        o_ref[pl.ds(r0, chunk), :] = (d * lax.rsqrt(var + LN_EPS) * g + b).astype(o_ref.dtype)
        return carry

    lax.fori_loop(0, rows // chunk, body, 0)


def _proj_kernel(x_ref, wu_ref, wg_ref, u_ref, gg_ref):
    x = x_ref[...]
    u_ref[...] = jnp.dot(x, wu_ref[...], preferred_element_type=F32)
    g = jnp.dot(x, wg_ref[...], preferred_element_type=F32)
    gg_ref[...] = jax.nn.gelu(g, approximate=True).astype(gg_ref.dtype)


def _proj(xb, w_in, tm, tn):
    t, d = xb.shape
    w = w_in.shape[1] // 2
    nj = w // tn
    return pl.pallas_call(
        _proj_kernel,
        grid=(t // tm, nj),
        in_specs=[
            pl.BlockSpec((tm, d), lambda i, j: (i, 0)),
            pl.BlockSpec((d, tn), lambda i, j: (0, j)),
            pl.BlockSpec((d, tn), lambda i, j: (0, j + nj)),
        ],
        out_specs=[
            pl.BlockSpec((tm, tn), lambda i, j: (i, j)),
            pl.BlockSpec((tm, tn), lambda i, j: (i, j)),
        ],
        out_shape=[jax.ShapeDtypeStruct((t, w), F32), jax.ShapeDtypeStruct((t, w), BF16)],
        compiler_params=_params("parallel", "arbitrary"),
        name="l0_proj",
    )(xb, w_in, w_in)


def _lru_kernel(u_ref, gg_ref, cw_ref, cb_ref, wr_ref, br_ref, wi_ref, bi_ref, lam_ref,
                hg_ref, ext_ref, a_ref, b_ref, h_ref, *, ts, n_blocks, block_w, scan_w):
    halo = 8
    s = pl.program_id(1)

    @pl.when(s == 0)
    def _():
        ext_ref[0:halo, :] = jnp.zeros((halo, ext_ref.shape[1]), F32)
        h_ref[...] = jnp.zeros_like(h_ref)

    ext_ref[halo:halo + ts, :] = u_ref[...]

    for hd in range(n_blocks):
        cs = slice(hd * block_w, (hd + 1) * block_w)
        uc = cb_ref[:, cs]
        for k in range(CONV_WIDTH):
            off = halo - (CONV_WIDTH - 1) + k
            uc = uc + cw_ref[k:k + 1, cs] * ext_ref[off:off + ts, cs]
        ucb = uc.astype(BF16)
        r = jax.nn.sigmoid(jnp.dot(ucb, wr_ref[hd], preferred_element_type=F32) + br_ref[:, cs])
        ig = jax.nn.sigmoid(jnp.dot(ucb, wi_ref[hd], preferred_element_type=F32) + bi_ref[:, cs])
        log_a = -LRU_C * r * jax.nn.softplus(-lam_ref[:, cs])
        a = jnp.exp(log_a)
        a_ref[:, cs] = a
        b_ref[:, cs] = jnp.sqrt(-jnp.tanh(log_a) * (a * a + 1.0)) * (ig * uc)

    ext_ref[0:halo, :] = ext_ref[ts:ts + halo, :]

    width = a_ref.shape[1]
    for c in range(width // scan_w):
        cs = pl.ds(c * scan_w, scan_w)

        def step(t, h, cs=cs):
            h = a_ref[pl.ds(t, 1), cs] * h + b_ref[pl.ds(t, 1), cs]
            b_ref[pl.ds(t, 1), cs] = h
            return h

        h_ref[0:1, cs] = lax.fori_loop(0, ts, step, h_ref[0:1, cs], unroll=8)

    hg_ref[...] = (b_ref[...] * gg_ref[...].astype(F32)).astype(hg_ref.dtype)


def _lru(u, gg, conv_w, conv_b, w_r, b_r, w_i, b_i, lam, batch, ts):
    t, w = u.shape
    n_blocks, block_w, _ = w_r.shape
    ns = t // batch // ts
    row = lambda b, s: (b * ns + s, 0)
    full2 = lambda b, s: (0, 0)
    full3 = lambda b, s: (0, 0, 0)
    kern = functools.partial(_lru_kernel, ts=ts, n_blocks=n_blocks, block_w=block_w,
                             scan_w=min(w, 1024))
    return pl.pallas_call(
        kern,
        grid=(batch, ns),
        in_specs=[
            pl.BlockSpec((ts, w), row),
            pl.BlockSpec((ts, w), row),
            pl.BlockSpec((CONV_WIDTH, w), full2),
            pl.BlockSpec((1, w), full2),
            pl.BlockSpec((n_blocks, block_w, block_w), full3),
            pl.BlockSpec((1, w), full2),
            pl.BlockSpec((n_blocks, block_w, block_w), full3),
            pl.BlockSpec((1, w), full2),
            pl.BlockSpec((1, w), full2),
        ],
        out_specs=pl.BlockSpec((ts, w), row),
        out_shape=jax.ShapeDtypeStruct((t, w), BF16),
        scratch_shapes=[
            pltpu.VMEM((ts + 8, w), F32),
            pltpu.VMEM((ts, w), F32),
            pltpu.VMEM((ts, w), F32),
            pltpu.VMEM((8, w), F32),
        ],
        compiler_params=_params("arbitrary", "arbitrary"),
        name="l0_rglru",
    )(u, gg, conv_w, conv_b.reshape(1, w), w_r, b_r.reshape(1, w), w_i, b_i.reshape(1, w),
      lam.reshape(1, w))


def _mm_ln_kernel(a_ref, w_ref, r_ref, g_ref, b_ref, *rest, nk, alpha, n_experts):
    if n_experts:
        wr_ref, br_ref, o_ref, route_ref, acc_ref = rest
    else:
        o_ref, acc_ref = rest
    k = pl.program_id(1)

    @pl.when(k == 0)
    def _():
        acc_ref[...] = alpha * r_ref[...].astype(F32)

    acc_ref[...] += jnp.dot(a_ref[...], w_ref[...], preferred_element_type=F32)

    @pl.when(k == nk - 1)
    def _():
        _layer_norm_rows(acc_ref, o_ref, g_ref, b_ref, acc_ref.shape[0])
        if n_experts:
            logits = jnp.dot(o_ref[...], wr_ref[...], preferred_element_type=F32) + br_ref[...]
            route_ref[...] = _top2_route(logits, n_experts)


def _top2_route(logits, n_experts):
    lane = lax.broadcasted_iota(jnp.int32, logits.shape, 1)
    neg = jnp.float32(-jnp.inf)
    lg = jnp.where(lane < n_experts, logits, neg)
    m1 = jnp.max(lg, axis=-1, keepdims=True)
    i1 = jnp.min(jnp.where(lg == m1, lane, LANES), axis=-1, keepdims=True)
    lg2 = jnp.where(lane == i1, neg, lg)
    m2 = jnp.max(lg2, axis=-1, keepdims=True)
    i2 = jnp.min(jnp.where(lg2 == m2, lane, LANES), axis=-1, keepdims=True)
    e = jnp.exp(m2 - m1)
    w1 = 1.0 / (1.0 + e)
    w2 = e / (1.0 + e)
    out = jnp.where(lane == 0, i1.astype(F32), 0.0)
    out = jnp.where(lane == 1, i2.astype(F32), out)
    out = jnp.where(lane == 2, w1, out)
    out = jnp.where(lane == 3, w2, out)
    return out


def _mm_ln(a, w, resid, g, b, alpha, tm, tk, router=None):
    t, kdim = a.shape
    n = w.shape[1]
    nk = kdim // tk
    n_experts = 0
    in_specs = [
        pl.BlockSpec((tm, tk), lambda i, k: (i, k)),
        pl.BlockSpec((tk, n), lambda i, k: (k, 0)),
        pl.BlockSpec((tm, n), lambda i, k: (i, 0)),
        pl.BlockSpec((1, n), lambda i, k: (0, 0)),
        pl.BlockSpec((1, n), lambda i, k: (0, 0)),
    ]
    args = [a, w, resid, g.reshape(1, n), b.reshape(1, n)]
    out_specs = [pl.BlockSpec((tm, n), lambda i, k: (i, 0))]
    out_shape = [jax.ShapeDtypeStruct((t, n), BF16)]
    if router is not None:
        w_router, b_router = router
        n_experts = w_router.shape[1]
        wr = jnp.zeros((n, LANES), BF16).at[:, :n_experts].set(w_router.astype(BF16))
        br = jnp.zeros((1, LANES), F32).at[0, :n_experts].set(b_router.astype(F32))
        in_specs += [pl.BlockSpec((n, LANES), lambda i, k: (0, 0)),
                     pl.BlockSpec((1, LANES), lambda i, k: (0, 0))]
        args += [wr, br]
        out_specs.append(pl.BlockSpec((tm, LANES), lambda i, k: (i, 0)))
        out_shape.append(jax.ShapeDtypeStruct((t, LANES), F32))
    kern = functools.partial(_mm_ln_kernel, nk=nk, alpha=alpha, n_experts=n_experts)
    out = pl.pallas_call(
        kern,
        grid=(t // tm, nk),
        in_specs=in_specs,
        out_specs=out_specs,
        out_shape=out_shape,
        scratch_shapes=[pltpu.VMEM((tm, n), F32)],
        compiler_params=_params("parallel", "arbitrary"),
        name="mm_res_ln",
    )(*args)
    return out if router is not None else out[0]


def _swiglu_up_kernel(x_ref, wg_ref, wu_ref, h_ref):
    x = x_ref[...]
    g = jnp.dot(x, wg_ref[...], preferred_element_type=F32)
    u = jnp.dot(x, wu_ref[...], preferred_element_type=F32)
    h_ref[...] = (jax.nn.silu(g) * u).astype(h_ref.dtype)


def _swiglu_up(xb, w_gate_up, tm, tn):
    t, d = xb.shape
    f = w_gate_up.shape[1] // 2
    nj = f // tn
    return pl.pallas_call(
        _swiglu_up_kernel,
        grid=(t // tm, nj),
        in_specs=[
            pl.BlockSpec((tm, d), lambda i, j: (i, 0)),
            pl.BlockSpec((d, tn), lambda i, j: (0, j)),
            pl.BlockSpec((d, tn), lambda i, j: (0, j + nj)),
        ],
        out_specs=pl.BlockSpec((tm, tn), lambda i, j: (i, j)),
        out_shape=jax.ShapeDtypeStruct((t, f), BF16),
        compiler_params=_params("parallel", "arbitrary"),
        name="swiglu_up",
    )(xb, w_gate_up, w_gate_up)


def _qkv_kernel(x_ref, w_ref, b_ref, cos_ref, sin_ref, o_ref, *, n_rope_tiles):
    j = pl.program_id(1)
    acc = jnp.dot(x_ref[...], w_ref[...], preferred_element_type=F32) + b_ref[...]

    @pl.when(j < n_rope_tiles)
    def _():
        cos = cos_ref[...]
        sin = sin_ref[...]
        for c in range(acc.shape[1] // HEAD_DIM):
            cs = slice(c * HEAD_DIM, (c + 1) * HEAD_DIM)
            tt = acc[:, cs]
            o_ref[:, cs] = (tt * cos + pltpu.roll(tt, HEAD_DIM // 2, 1) * sin).astype(o_ref.dtype)

    @pl.when(j >= n_rope_tiles)
    def _():
        o_ref[...] = acc.astype(o_ref.dtype)


def _qkv(xb, w_qkv, b_qkv, cos_f, sin_f, n_rope_cols, tm, tn):
    t, d = xb.shape
    n = w_qkv.shape[1]
    seq = cos_f.shape[0]
    n_pos_tiles = seq // tm
    kern = functools.partial(_qkv_kernel, n_rope_tiles=n_rope_cols // tn)
    return pl.pallas_call(
        kern,
        grid=(t // tm, n // tn),
        in_specs=[
            pl.BlockSpec((tm, d), lambda i, j: (i, 0)),
            pl.BlockSpec((d, tn), lambda i, j: (0, j)),
            pl.BlockSpec((1, tn), lambda i, j: (0, j)),
            pl.BlockSpec((tm, HEAD_DIM), lambda i, j: (i % n_pos_tiles, 0)),
            pl.BlockSpec((tm, HEAD_DIM), lambda i, j: (i % n_pos_tiles, 0)),
        ],
        out_specs=pl.BlockSpec((tm, tn), lambda i, j: (i, j)),
        out_shape=jax.ShapeDtypeStruct((t, n), BF16),
        compiler_params=_params("parallel", "arbitrary"),
        name="l1_qkv_rope",
    )(xb, w_qkv, b_qkv.reshape(1, n), cos_f, sin_f)


def _swa_kernel(sink_ref, q_ref, k_ref, v_ref, kp_ref, vp_ref, o_ref, *, rows, n_kv, group,
                tiles_per_seq):
    i = pl.program_id(0)
    nqb = rows // WINDOW
    gq = group * WINDOW
    scale = HEAD_DIM ** -0.5
    neg = jnp.float32(-jnp.inf)

    qi = lax.broadcasted_iota(jnp.int32, (gq, 2 * WINDOW), 0) % WINDOW
    kj = lax.broadcasted_iota(jnp.int32, (gq, 2 * WINDOW), 1)
    delta = qi + WINDOW - kj
    band = (delta >= 0) & (delta < WINDOW)
    bias = jnp.where(band, 0.0, neg)
    seq_start = (i % tiles_per_seq) == 0
    bias_first = jnp.where(jnp.logical_and(seq_start, kj < WINDOW), neg, bias)
    grp = lax.broadcasted_iota(jnp.int32, (gq, 1), 0) // WINDOW

    for hk in range(n_kv):
        ks = slice(hk * HEAD_DIM, (hk + 1) * HEAD_DIM)
        sink = jnp.zeros((gq, 1), F32)
        for g in range(group):
            sink = jnp.where(grp == g, sink_ref[hk * group + g], sink)
        for qb in range(nqb):
            rs = slice(qb * WINDOW, (qb + 1) * WINDOW)
            if qb == 0:
                k_prev, v_prev = kp_ref[:, ks], vp_ref[:, ks]
            else:
                ps = slice((qb - 1) * WINDOW, qb * WINDOW)
                k_prev, v_prev = k_ref[ps, ks], v_ref[ps, ks]
            k_win = jnp.concatenate([k_prev, k_ref[rs, ks]], axis=0)
            v_win = jnp.concatenate([v_prev, v_ref[rs, ks]], axis=0)
            q4 = jnp.concatenate(
                [q_ref[rs, (hk * group + g) * HEAD_DIM:(hk * group + g + 1) * HEAD_DIM]
                 for g in range(group)], axis=0)
            sc = lax.dot_general(q4, k_win, (((1,), (1,)), ((), ())),
                                 preferred_element_type=F32) * scale
            sc = sc + (bias_first if qb == 0 else bias)
            m = jnp.maximum(jnp.max(sc, axis=-1, keepdims=True), sink)
            p = jnp.exp(sc - m)
            denom = jnp.sum(p, axis=-1, keepdims=True) + jnp.exp(sink - m)
            o = jnp.dot((p / denom).astype(v_win.dtype), v_win, preferred_element_type=F32)
            for g in range(group):
                hs = slice((hk * group + g) * HEAD_DIM, (hk * group + g + 1) * HEAD_DIM)
                o_ref[rs, hs] = o[g * WINDOW:(g + 1) * WINDOW, :].astype(o_ref.dtype)


def _swa(qkv, sinks, n_q, n_kv, seq, rows):
    t = qkv.shape[0]
    dq = n_q * HEAD_DIM
    dkv = n_kv * HEAD_DIM
    assert dq % dkv == 0
    k_blk = dq // dkv
    nqb = rows // WINDOW
    kern = functools.partial(_swa_kernel, rows=rows, n_kv=n_kv, group=n_q // n_kv,
                             tiles_per_seq=seq // rows)
    prev = lambda c: (lambda i, s: (jnp.maximum(i * nqb - 1, 0), c))
    grid_spec = pltpu.PrefetchScalarGridSpec(
        num_scalar_prefetch=1,
        grid=(t // rows,),
        in_specs=[
            pl.BlockSpec((rows, dq), lambda i, s: (i, 0)),
            pl.BlockSpec((rows, dkv), lambda i, s: (i, k_blk)),
            pl.BlockSpec((rows, dkv), lambda i, s: (i, k_blk + 1)),
            pl.BlockSpec((WINDOW, dkv), prev(k_blk)),
            pl.BlockSpec((WINDOW, dkv), prev(k_blk + 1)),
        ],
        out_specs=pl.BlockSpec((rows, dq), lambda i, s: (i, 0)),
    )
    return pl.pallas_call(
        kern,
        grid_spec=grid_spec,
        out_shape=jax.ShapeDtypeStruct((t, dq), BF16),
        compiler_params=_params("arbitrary"),
        name="l1_swa",
    )(sinks.astype(F32), qkv, qkv, qkv, qkv, qkv)


def _moe_up_kernel(te_ref, tv_ref, x_ref, wg_ref, wu_ref, h_ref):
    valid = tv_ref[pl.program_id(0)] != 0

    @pl.when(valid)
    def _():
        x = x_ref[...]
        g = jnp.dot(x, wg_ref[0], preferred_element_type=F32)
        u = jnp.dot(x, wu_ref[0], preferred_element_type=F32)
        h_ref[...] = (jax.nn.silu(g) * u).astype(h_ref.dtype)

    @pl.when(jnp.logical_not(valid))
    def _():
        h_ref[...] = jnp.zeros_like(h_ref)


def _moe_up(tile_expert, tile_valid, xs, we_gate_up, tm, tn):
    p, d = xs.shape
    f = we_gate_up.shape[2] // 2
    nj = f // tn

    def wmap(off):
        return lambda i, j, te, tv: (te[i], 0, jnp.where(tv[i] != 0, j, nj - 1) + off)

    grid_spec = pltpu.PrefetchScalarGridSpec(
        num_scalar_prefetch=2,
        grid=(p // tm, nj),
        in_specs=[
            pl.BlockSpec((tm, d), lambda i, j, te, tv: (i, 0)),
            pl.BlockSpec((1, d, tn), wmap(0)),
            pl.BlockSpec((1, d, tn), wmap(nj)),
        ],
        out_specs=pl.BlockSpec((tm, tn), lambda i, j, te, tv: (i, j)),
    )
    return pl.pallas_call(
        _moe_up_kernel,
        grid_spec=grid_spec,
        out_shape=jax.ShapeDtypeStruct((p, f), BF16),
        compiler_params=_params("arbitrary", "arbitrary"),
        name="moe_up",
    )(tile_expert, tile_valid, xs, we_gate_up, we_gate_up)


def _moe_down_kernel(te_ref, tv_ref, h_ref, w_ref, y_ref):
    valid = tv_ref[pl.program_id(0)] != 0

    @pl.when(valid)
    def _():
        y_ref[...] = jnp.dot(h_ref[...], w_ref[0], preferred_element_type=F32).astype(y_ref.dtype)

    @pl.when(jnp.logical_not(valid))
    def _():
        y_ref[...] = jnp.zeros_like(y_ref)


def _moe_down(tile_expert, tile_valid, hs, we_down, tm, tn):
    p, f = hs.shape
    d = we_down.shape[2]
    nj = d // tn
    grid_spec = pltpu.PrefetchScalarGridSpec(
        num_scalar_prefetch=2,
        grid=(p // tm, nj),
        in_specs=[
            pl.BlockSpec((tm, f), lambda i, j, te, tv: (i, 0)),
            pl.BlockSpec((1, f, tn), lambda i, j, te, tv: (te[i], 0, jnp.where(tv[i] != 0, j, nj - 1))),
        ],
        out_specs=pl.BlockSpec((tm, tn), lambda i, j, te, tv: (i, j)),
    )
    return pl.pallas_call(
        _moe_down_kernel,
        grid_spec=grid_spec,
        out_shape=jax.ShapeDtypeStruct((p, d), BF16),
        compiler_params=_params("arbitrary", "arbitrary"),
        name="moe_down",
    )(tile_expert, tile_valid, hs, we_down)


def _moe_combine_kernel(x_ref, y1_ref, y2_ref, route_ref, g_ref, b_ref, o_ref, acc_ref, *, alpha):
    w1 = route_ref[:, 2:3]
    w2 = route_ref[:, 3:4]
    acc_ref[...] = (alpha * x_ref[...].astype(F32) + w1 * y1_ref[...].astype(F32)
                    + w2 * y2_ref[...].astype(F32))
    _layer_norm_rows(acc_ref, o_ref, g_ref, b_ref, acc_ref.shape[0], chunk=8)


def _moe_combine(xb, y1, y2, route, g, b, alpha, tm, out_dtype):
    t, d = xb.shape
    row = lambda i: (i, 0)
    kern = functools.partial(_moe_combine_kernel, alpha=alpha)
    return pl.pallas_call(
        kern,
        grid=(t // tm,),
        in_specs=[
            pl.BlockSpec((tm, d), row),
            pl.BlockSpec((tm, d), row),
            pl.BlockSpec((tm, d), row),
            pl.BlockSpec((tm, LANES), row),
            pl.BlockSpec((1, d), lambda i: (0, 0)),
            pl.BlockSpec((1, d), lambda i: (0, 0)),
        ],
        out_specs=pl.BlockSpec((tm, d), row),
        out_shape=jax.ShapeDtypeStruct((t, d), out_dtype),
        scratch_shapes=[pltpu.VMEM((tm, d), F32)],
        compiler_params=_params("parallel"),
        name="moe_combine_ln",
    )(xb, y1, y2, route, g.reshape(1, d), b.reshape(1, d))


def _route_plan(route, n_experts, tm):
    t = route.shape[0]
    e = route[:, :TOP_K].astype(jnp.int32).reshape(-1)
    onehot = (e[:, None] == jnp.arange(n_experts, dtype=jnp.int32)[None, :]).astype(jnp.int32)
    cum = jnp.cumsum(onehot, axis=0)
    rank = jnp.take_along_axis(cum, e[:, None], axis=1)[:, 0] - 1
    counts = cum[-1]
    padded = ((counts + tm - 1) // tm) * tm
    ends = jnp.cumsum(padded)
    pos = (ends - padded)[e] + rank
    p_rows = TOP_K * t + n_experts * tm
    row_token = jnp.zeros((p_rows,), jnp.int32).at[pos].set(jnp.arange(TOP_K * t, dtype=jnp.int32) // TOP_K)
    tile_start = jnp.arange(p_rows // tm, dtype=jnp.int32) * tm
    tile_valid = (tile_start < ends[-1]).astype(jnp.int32)
    last_start = jnp.minimum(tile_start, ends[-1] - tm)
    tile_expert = jnp.minimum(jnp.searchsorted(ends, last_start, side="right"), n_experts - 1).astype(jnp.int32)
    return pos.reshape(t, TOP_K), row_token, tile_expert, tile_valid


def _tile(dim, want):
    return min(dim, want)


def kernel(x, l0_w_in, l0_conv_w, l0_conv_b, l0_w_rgate, l0_b_rgate, l0_w_igate, l0_b_igate, l0_lru_lambda, l0_w_out, l0_ln1_g, l0_ln1_b, l0_w_gate_up, l0_w_down, l0_ln2_g, l0_ln2_b, l1_w_qkv, l1_b_qkv, l1_sinks, l1_w_o, l1_ln1_g, l1_ln1_b, l1_w_router, l1_b_router, l1_we_gate_up, l1_we_down, l1_ln2_g, l1_ln2_b):
    batch, seq, d = x.shape
    t = batch * seq
    depth = 2
    alpha = (2 * depth) ** 0.25
    n_q = l1_sinks.shape[0]
    n_kv = (l1_w_qkv.shape[1] // HEAD_DIM - n_q) // 2
    n_experts = l1_w_router.shape[1]

    x2d = x.reshape(t, d)
    xb = x2d.astype(BF16)
    bf = lambda w: w.astype(BF16)

    tm_big = _tile(t, 1024)
    tm_ln = _tile(t, 512)

    u, gg = _proj(xb, bf(l0_w_in), tm_big, _tile(d, 512))
    hg = _lru(u, gg, l0_conv_w, l0_conv_b, bf(l0_w_rgate), l0_b_rgate, bf(l0_w_igate), l0_b_igate,
              l0_lru_lambda, batch, _tile(seq, 256))
    x1 = _mm_ln(hg, bf(l0_w_out), x2d, l0_ln1_g, l0_ln1_b, alpha, tm_ln, _tile(d, 1024))
    h = _swiglu_up(x1, bf(l0_w_gate_up), tm_big, _tile(l0_w_gate_up.shape[1] // 2, 512))
    x2 = _mm_ln(h, bf(l0_w_down), x1, l0_ln2_g, l0_ln2_b, alpha, tm_ln, _tile(h.shape[1], 1024))

    pos = jnp.arange(seq, dtype=F32)
    inv = ROPE_THETA ** (-jnp.arange(0, HEAD_DIM, 2, dtype=F32) / HEAD_DIM)
    ang = pos[:, None] * inv[None, :]
    cos_f = jnp.concatenate([jnp.cos(ang), jnp.cos(ang)], axis=-1)
    sin_f = jnp.concatenate([-jnp.sin(ang), jnp.sin(ang)], axis=-1)
    qkv = _qkv(x2, bf(l1_w_qkv), l1_b_qkv, cos_f, sin_f, (n_q + n_kv) * HEAD_DIM,
               _tile(seq, 1024), _tile(n_kv * HEAD_DIM, 512))
    o = _swa(qkv, l1_sinks, n_q, n_kv, seq, _tile(seq, 512))
    x3, route = _mm_ln(o, bf(l1_w_o), x2, l1_ln1_g, l1_ln1_b, alpha, tm_ln, _tile(d, 1024),
                       router=(l1_w_router, l1_b_router))

    tm_e = _tile(t, 512)
    pos2, row_token, tile_expert, tile_valid = _route_plan(route, n_experts, tm_e)
    xs = x3[row_token]
    f_e = l1_we_gate_up.shape[2] // 2
    hs = _moe_up(tile_expert, tile_valid, xs, bf(l1_we_gate_up), tm_e, _tile(f_e, 512))
    ys = _moe_down(tile_expert, tile_valid, hs, bf(l1_we_down), tm_e, _tile(d, 1024))
    out = _moe_combine(x3, ys[pos2[:, 0]], ys[pos2[:, 1]], route, l1_ln2_g, l1_ln2_b, alpha,
                       _tile(t, 256), x.dtype)
    return out.reshape(batch, seq, d)
```

```python
import functools

import jax
import jax.numpy as jnp
from jax import lax
from jax.experimental import pallas as pl
from jax.experimental.pallas import tpu as pltpu

F32 = jnp.float32
BF16 = jnp.bfloat16

HEAD_DIM = 128
WINDOW = 128
CONV_WIDTH = 4
LRU_C = 8.0
TOP_K = 2
ROPE_THETA = 10000.0
LN_EPS = 1e-5
LANES = 128
VMEM_LIMIT_BYTES = 56 * 1024 * 1024
LN_SLAB = 64
LN_CHUNK = 16


def _params(*sem):
    return pltpu.CompilerParams(dimension_semantics=sem, vmem_limit_bytes=VMEM_LIMIT_BYTES)


def _ln_finish(y_ref, mu_ref, rstd_ref, o_ref, g_ref, b_ref):
    rows = y_ref.shape[0]
    inv_n = 1.0 / y_ref.shape[1]
    for s0 in range(0, rows, LN_SLAB):
        rs = slice(s0, s0 + LN_SLAB)
        d = y_ref[rs, :] - mu_ref[rs, :]
        var = jnp.sum(d * d, axis=-1, keepdims=True) * inv_n
        rstd_ref[rs, :] = lax.rsqrt(var + LN_EPS)

    def body(c, carry):
        r0 = pl.multiple_of(c * LN_CHUNK, LN_CHUNK)
        rr = pl.ds(r0, LN_CHUNK)
        o = (y_ref[rr, :] - mu_ref[rr, :]) * rstd_ref[rr, :] * g_ref[...] + b_ref[...]
        o_ref[rr, :] = o.astype(o_ref.dtype)
        return carry

    lax.fori_loop(0, rows // LN_CHUNK, body, 0, unroll=2)


def _load_weights(first, w_refs, wb_refs):
    @pl.when(first)
    def _():
        for w_ref, wb_ref in zip(w_refs, wb_refs):
            w = w_ref[0] if len(w_ref.shape) == 3 else w_ref[...]
            wb_ref[...] = w.astype(wb_ref.dtype)


def _proj_kernel(x_ref, wu_ref, wg_ref, u_ref, gg_ref, wub_ref, wgb_ref):
    _load_weights(pl.program_id(1) == 0, (wu_ref, wg_ref), (wub_ref, wgb_ref))
    x = x_ref[...]
    u_ref[...] = jnp.dot(x, wub_ref[...], preferred_element_type=F32)
    g = jnp.dot(x, wgb_ref[...], preferred_element_type=F32)
    gg_ref[...] = jax.nn.gelu(g, approximate=True).astype(gg_ref.dtype)


def _proj(xb, w_in, tm, tn):
    t, d = xb.shape
    w = w_in.shape[1] // 2
    nj = w // tn
    return pl.pallas_call(
        _proj_kernel,
        grid=(nj, t // tm),
        in_specs=[
            pl.BlockSpec((tm, d), lambda j, i: (i, 0)),
            pl.BlockSpec((d, tn), lambda j, i: (0, j)),
            pl.BlockSpec((d, tn), lambda j, i: (0, j + nj)),
        ],
        out_specs=[
            pl.BlockSpec((tm, tn), lambda j, i: (i, j)),
            pl.BlockSpec((tm, tn), lambda j, i: (i, j)),
        ],
        out_shape=[jax.ShapeDtypeStruct((t, w), F32), jax.ShapeDtypeStruct((t, w), BF16)],
        scratch_shapes=[pltpu.VMEM((d, tn), BF16), pltpu.VMEM((d, tn), BF16)],
        compiler_params=_params("arbitrary", "arbitrary"),
        name="l0_proj",
    )(xb, w_in, w_in)


def _swiglu_up_kernel(x_ref, wg_ref, wu_ref, h_ref, wgb_ref, wub_ref):
    _load_weights(pl.program_id(1) == 0, (wg_ref, wu_ref), (wgb_ref, wub_ref))
    x = x_ref[...]
    g = jnp.dot(x, wgb_ref[...], preferred_element_type=F32)
    u = jnp.dot(x, wub_ref[...], preferred_element_type=F32)
    h_ref[...] = (jax.nn.silu(g) * u).astype(h_ref.dtype)


def _swiglu_up(xb, w_gate_up, tm, tn):
    t, d = xb.shape
    f = w_gate_up.shape[1] // 2
    nj = f // tn
    return pl.pallas_call(
        _swiglu_up_kernel,
        grid=(nj, t // tm),
        in_specs=[
            pl.BlockSpec((tm, d), lambda j, i: (i, 0)),
            pl.BlockSpec((d, tn), lambda j, i: (0, j)),
            pl.BlockSpec((d, tn), lambda j, i: (0, j + nj)),
        ],
        out_specs=pl.BlockSpec((tm, tn), lambda j, i: (i, j)),
        out_shape=jax.ShapeDtypeStruct((t, f), BF16),
        scratch_shapes=[pltpu.VMEM((d, tn), BF16), pltpu.VMEM((d, tn), BF16)],
        compiler_params=_params("arbitrary", "arbitrary"),
        name="swiglu_up",
    )(xb, w_gate_up, w_gate_up)


def _qkv_kernel(x_ref, w_ref, b_ref, cos_ref, sin_ref, o_ref, wb_ref, *, n_rope_tiles):
    _load_weights(pl.program_id(1) == 0, (w_ref,), (wb_ref,))
    acc = jnp.dot(x_ref[...], wb_ref[...], preferred_element_type=F32) + b_ref[...]
    rope = pl.program_id(0) < n_rope_tiles
    cos = jnp.where(rope, cos_ref[...], 1.0)
    sin = jnp.where(rope, sin_ref[...], 0.0)
    for c in range(acc.shape[1] // HEAD_DIM):
        cs = slice(c * HEAD_DIM, (c + 1) * HEAD_DIM)
        tt = acc[:, cs]
        o_ref[:, cs] = (tt * cos + pltpu.roll(tt, HEAD_DIM // 2, 1) * sin).astype(o_ref.dtype)


def _qkv(xb, w_qkv, b_qkv, cos_f, sin_f, n_rope_cols, tm, tn):
    t, d = xb.shape
    n = w_qkv.shape[1]
    seq = cos_f.shape[0]
    n_pos_tiles = seq // tm
    kern = functools.partial(_qkv_kernel, n_rope_tiles=n_rope_cols // tn)
    return pl.pallas_call(
        kern,
        grid=(n // tn, t // tm),
        in_specs=[
            pl.BlockSpec((tm, d), lambda j, i: (i, 0)),
            pl.BlockSpec((d, tn), lambda j, i: (0, j)),
            pl.BlockSpec((1, tn), lambda j, i: (0, j)),
            pl.BlockSpec((tm, HEAD_DIM), lambda j, i: (i % n_pos_tiles, 0)),
            pl.BlockSpec((tm, HEAD_DIM), lambda j, i: (i % n_pos_tiles, 0)),
        ],
        out_specs=pl.BlockSpec((tm, tn), lambda j, i: (i, j)),
        out_shape=jax.ShapeDtypeStruct((t, n), BF16),
        scratch_shapes=[pltpu.VMEM((d, tn), BF16)],
        compiler_params=_params("arbitrary", "arbitrary"),
        name="l1_qkv_rope",
    )(xb, w_qkv, b_qkv.reshape(1, n), cos_f, sin_f)


def _lru_kernel(u_ref, gg_ref, cw_ref, cb_ref, wr_ref, br_ref, wi_ref, bi_ref, lam_ref,
                hg_ref, ext_ref, a_ref, b_ref, h_ref, *, ts, n_blocks, block_w, scan_w):
    halo = 8
    s = pl.program_id(1)

    @pl.when(s == 0)
    def _():
        ext_ref[0:halo, :] = jnp.zeros((halo, ext_ref.shape[1]), F32)
        h_ref[...] = jnp.zeros_like(h_ref)

    ext_ref[halo:halo + ts, :] = u_ref[...]

    for hd in range(n_blocks):
        cs = slice(hd * block_w, (hd + 1) * block_w)
        uc = cb_ref[:, cs]
        for k in range(CONV_WIDTH):
            off = halo - (CONV_WIDTH - 1) + k
            uc = uc + cw_ref[k:k + 1, cs] * ext_ref[off:off + ts, cs]
        ucb = uc.astype(BF16)
        r = jax.nn.sigmoid(jnp.dot(ucb, wr_ref[hd], preferred_element_type=F32) + br_ref[:, cs])
        ig = jax.nn.sigmoid(jnp.dot(ucb, wi_ref[hd], preferred_element_type=F32) + bi_ref[:, cs])
        log_a = -LRU_C * r * jax.nn.softplus(-lam_ref[:, cs])
        a = jnp.exp(log_a)
        a_ref[:, cs] = a
        b_ref[:, cs] = jnp.sqrt(-jnp.tanh(log_a) * (a * a + 1.0)) * (ig * uc)

    ext_ref[0:halo, :] = ext_ref[ts:ts + halo, :]

    width = a_ref.shape[1]
    for c in range(width // scan_w):
        cs = pl.ds(c * scan_w, scan_w)

        def step(t, h, cs=cs):
            h = a_ref[pl.ds(t, 1), cs] * h + b_ref[pl.ds(t, 1), cs]
            b_ref[pl.ds(t, 1), cs] = h
            return h

        h_ref[0:1, cs] = lax.fori_loop(0, ts, step, h_ref[0:1, cs], unroll=8)

    hg_ref[...] = (b_ref[...] * gg_ref[...].astype(F32)).astype(hg_ref.dtype)


def _lru(u, gg, conv_w, conv_b, w_r, b_r, w_i, b_i, lam, batch, ts):
    t, w = u.shape
    n_blocks, block_w, _ = w_r.shape
    ns = t // batch // ts
    row = lambda b, s: (b * ns + s, 0)
    full2 = lambda b, s: (0, 0)
    full3 = lambda b, s: (0, 0, 0)
    kern = functools.partial(_lru_kernel, ts=ts, n_blocks=n_blocks, block_w=block_w,
                             scan_w=min(w, 1024))
    return pl.pallas_call(
        kern,
        grid=(batch, ns),
        in_specs=[
            pl.BlockSpec((ts, w), row),
            pl.BlockSpec((ts, w), row),
            pl.BlockSpec((CONV_WIDTH, w), full2),
            pl.BlockSpec((1, w), full2),
            pl.BlockSpec((n_blocks, block_w, block_w), full3),
            pl.BlockSpec((1, w), full2),
            pl.BlockSpec((n_blocks, block_w, block_w), full3),
            pl.BlockSpec((1, w), full2),
            pl.BlockSpec((1, w), full2),
        ],
        out_specs=pl.BlockSpec((ts, w), row),
        out_shape=jax.ShapeDtypeStruct((t, w), BF16),
        scratch_shapes=[
            pltpu.VMEM((ts + 8, w), F32),
            pltpu.VMEM((ts, w), F32),
            pltpu.VMEM((ts, w), F32),
            pltpu.VMEM((8, w), F32),
        ],
        compiler_params=_params("arbitrary", "arbitrary"),
        name="l0_rglru",
    )(u, gg, conv_w, conv_b.reshape(1, w), w_r, b_r.reshape(1, w), w_i, b_i.reshape(1, w),
      lam.reshape(1, w))


def _mm_ln_kernel(a_ref, w_ref, r_ref, g_ref, b_ref, *rest, nk, alpha, n_experts):
    if n_experts:
        wr_ref, br_ref, o_ref, route_ref, acc_ref, mu_ref, rstd_ref = rest
    else:
        o_ref, acc_ref, mu_ref, rstd_ref = rest
    k = pl.program_id(1)

    @pl.when(k == 0)
    def _():
        acc_ref[...] = jnp.dot(a_ref[...], w_ref[...], preferred_element_type=F32)

    @pl.when(jnp.logical_and(k > 0, k < nk - 1))
    def _():
        acc_ref[...] += jnp.dot(a_ref[...], w_ref[...], preferred_element_type=F32)

    @pl.when(k == nk - 1)
    def _():
        y = (acc_ref[...] + jnp.dot(a_ref[...], w_ref[...], preferred_element_type=F32)
             + alpha * r_ref[...].astype(F32))
        acc_ref[...] = y
        mu_ref[...] = jnp.mean(y, axis=-1, keepdims=True)
        _ln_finish(acc_ref, mu_ref, rstd_ref, o_ref, g_ref, b_ref)
        if n_experts:
            logits = jnp.dot(o_ref[...], wr_ref[...], preferred_element_type=F32) + br_ref[...]
            route_ref[...] = _top2_route(logits, n_experts)


def _top2_route(logits, n_experts):
    lane = lax.broadcasted_iota(jnp.int32, logits.shape, 1)
    neg = jnp.float32(-jnp.inf)
    lg = jnp.where(lane < n_experts, logits, neg)
    m1 = jnp.max(lg, axis=-1, keepdims=True)
    i1 = jnp.min(jnp.where(lg == m1, lane, LANES), axis=-1, keepdims=True)
    lg2 = jnp.where(lane == i1, neg, lg)
    m2 = jnp.max(lg2, axis=-1, keepdims=True)
    i2 = jnp.min(jnp.where(lg2 == m2, lane, LANES), axis=-1, keepdims=True)
    e = jnp.exp(m2 - m1)
    w1 = 1.0 / (1.0 + e)
    w2 = e / (1.0 + e)
    out = jnp.where(lane == 0, i1.astype(F32), 0.0)
    out = jnp.where(lane == 1, i2.astype(F32), out)
    out = jnp.where(lane == 2, w1, out)
    out = jnp.where(lane == 3, w2, out)
    return out


def _mm_ln(a, w, resid, g, b, alpha, tm, tk, router=None):
    t, kdim = a.shape
    n = w.shape[1]
    nk = kdim // tk
    assert nk >= 2
    n_experts = 0
    in_specs = [
        pl.BlockSpec((tm, tk), lambda i, k: (i, k)),
        pl.BlockSpec((tk, n), lambda i, k: (k, 0)),
        pl.BlockSpec((tm, n), lambda i, k: (i, 0)),
        pl.BlockSpec((1, n), lambda i, k: (0, 0)),
        pl.BlockSpec((1, n), lambda i, k: (0, 0)),
    ]
    args = [a, w, resid, g.reshape(1, n), b.reshape(1, n)]
    out_specs = [pl.BlockSpec((tm, n), lambda i, k: (i, 0))]
    out_shape = [jax.ShapeDtypeStruct((t, n), BF16)]
    if router is not None:
        w_router, b_router = router
        n_experts = w_router.shape[1]
        wr = jnp.zeros((n, LANES), BF16).at[:, :n_experts].set(w_router.astype(BF16))
        br = jnp.zeros((1, LANES), F32).at[0, :n_experts].set(b_router.astype(F32))
        in_specs += [pl.BlockSpec((n, LANES), lambda i, k: (0, 0)),
                     pl.BlockSpec((1, LANES), lambda i, k: (0, 0))]
        args += [wr, br]
        out_specs.append(pl.BlockSpec((tm, LANES), lambda i, k: (i, 0)))
        out_shape.append(jax.ShapeDtypeStruct((t, LANES), F32))
    kern = functools.partial(_mm_ln_kernel, nk=nk, alpha=alpha, n_experts=n_experts)
    out = pl.pallas_call(
        kern,
        grid=(t // tm, nk),
        in_specs=in_specs,
        out_specs=out_specs,
        out_shape=out_shape,
        scratch_shapes=[pltpu.VMEM((tm, n), F32), pltpu.VMEM((tm, 1), F32), pltpu.VMEM((tm, 1), F32)],
        compiler_params=_params("parallel", "arbitrary"),
        name="mm_res_ln",
    )(*args)
    return out if router is not None else out[0]


def _swa_kernel(sink_ref, q_ref, k_ref, v_ref, kp_ref, vp_ref, o_ref, *, rows, n_kv, group,
                tiles_per_seq):
    i = pl.program_id(0)
    nqb = rows // WINDOW
    gq = group * WINDOW
    scale = HEAD_DIM ** -0.5
    neg = jnp.float32(-jnp.inf)

    qi = lax.broadcasted_iota(jnp.int32, (gq, 2 * WINDOW), 0) % WINDOW
    kj = lax.broadcasted_iota(jnp.int32, (gq, 2 * WINDOW), 1)
    delta = qi + WINDOW - kj
    band = (delta >= 0) & (delta < WINDOW)
    bias = jnp.where(band, 0.0, neg)
    seq_start = (i % tiles_per_seq) == 0
    bias_first = jnp.where(jnp.logical_and(seq_start, kj < WINDOW), neg, bias)
    grp = lax.broadcasted_iota(jnp.int32, (gq, 1), 0) // WINDOW

    for hk in range(n_kv):
        ks = slice(hk * HEAD_DIM, (hk + 1) * HEAD_DIM)
        sink = jnp.zeros((gq, 1), F32)
        for g in range(group):
            sink = jnp.where(grp == g, sink_ref[hk * group + g], sink)
        for qb in range(nqb):
            rs = slice(qb * WINDOW, (qb + 1) * WINDOW)
            if qb == 0:
                k_prev, v_prev = kp_ref[:, ks], vp_ref[:, ks]
            else:
                ps = slice((qb - 1) * WINDOW, qb * WINDOW)
                k_prev, v_prev = k_ref[ps, ks], v_ref[ps, ks]
            k_win = jnp.concatenate([k_prev, k_ref[rs, ks]], axis=0)
            v_win = jnp.concatenate([v_prev, v_ref[rs, ks]], axis=0)
            q4 = jnp.concatenate(
                [q_ref[rs, (hk * group + g) * HEAD_DIM:(hk * group + g + 1) * HEAD_DIM]
                 for g in range(group)], axis=0)
            sc = lax.dot_general(q4, k_win, (((1,), (1,)), ((), ())),
                                 preferred_element_type=F32) * scale
            sc = sc + (bias_first if qb == 0 else bias)
            m = jnp.maximum(jnp.max(sc, axis=-1, keepdims=True), sink)
            p = jnp.exp(sc - m)
            denom = jnp.sum(p, axis=-1, keepdims=True) + jnp.exp(sink - m)
            o = jnp.dot((p / denom).astype(v_win.dtype), v_win, preferred_element_type=F32)
            for g in range(group):
                hs = slice((hk * group + g) * HEAD_DIM, (hk * group + g + 1) * HEAD_DIM)
                o_ref[rs, hs] = o[g * WINDOW:(g + 1) * WINDOW, :].astype(o_ref.dtype)


def _swa(qkv, sinks, n_q, n_kv, seq, rows):
    t = qkv.shape[0]
    dq = n_q * HEAD_DIM
    dkv = n_kv * HEAD_DIM
    assert dq % dkv == 0
    k_blk = dq // dkv
    nqb = rows // WINDOW
    kern = functools.partial(_swa_kernel, rows=rows, n_kv=n_kv, group=n_q // n_kv,
                             tiles_per_seq=seq // rows)
    prev = lambda c: (lambda i, s: (jnp.maximum(i * nqb - 1, 0), c))
    grid_spec = pltpu.PrefetchScalarGridSpec(
        num_scalar_prefetch=1,
        grid=(t // rows,),
        in_specs=[
            pl.BlockSpec((rows, dq), lambda i, s: (i, 0)),
            pl.BlockSpec((rows, dkv), lambda i, s: (i, k_blk)),
            pl.BlockSpec((rows, dkv), lambda i, s: (i, k_blk + 1)),
            pl.BlockSpec((WINDOW, dkv), prev(k_blk)),
            pl.BlockSpec((WINDOW, dkv), prev(k_blk + 1)),
        ],
        out_specs=pl.BlockSpec((rows, dq), lambda i, s: (i, 0)),
    )
    return pl.pallas_call(
        kern,
        grid_spec=grid_spec,
        out_shape=jax.ShapeDtypeStruct((t, dq), BF16),
        compiler_params=_params("arbitrary"),
        name="l1_swa",
    )(sinks.astype(F32), qkv, qkv, qkv, qkv, qkv)


def _new_expert(te_ref):
    i = pl.program_id(1)
    return jnp.logical_or(i == 0, te_ref[i] != te_ref[jnp.maximum(i - 1, 0)])


def _moe_up_kernel(te_ref, nv_ref, x_ref, wg_ref, wu_ref, h_ref, wgb_ref, wub_ref):
    _load_weights(_new_expert(te_ref), (wg_ref, wu_ref), (wgb_ref, wub_ref))
    valid = pl.program_id(1) < nv_ref[0]

    @pl.when(valid)
    def _():
        x = x_ref[...]
        g = jnp.dot(x, wgb_ref[...], preferred_element_type=F32)
        u = jnp.dot(x, wub_ref[...], preferred_element_type=F32)
        h_ref[...] = (jax.nn.silu(g) * u).astype(h_ref.dtype)

    @pl.when(jnp.logical_not(valid))
    def _():
        h_ref[...] = jnp.zeros_like(h_ref)


def _moe_up(tile_expert, n_valid, xs, we_gate_up, tm, tn):
    p, d = xs.shape
    f = we_gate_up.shape[2] // 2
    nj = f // tn
    row = lambda j, i, te, nv: (jnp.minimum(i, nv[0] - 1), 0)
    grid_spec = pltpu.PrefetchScalarGridSpec(
        num_scalar_prefetch=2,
        grid=(nj, p // tm),
        in_specs=[
            pl.BlockSpec((tm, d), row),
            pl.BlockSpec((1, d, tn), lambda j, i, te, nv: (te[i], 0, j)),
            pl.BlockSpec((1, d, tn), lambda j, i, te, nv: (te[i], 0, j + nj)),
        ],
        out_specs=pl.BlockSpec((tm, tn), lambda j, i, te, nv: (i, j)),
        scratch_shapes=[pltpu.VMEM((d, tn), BF16), pltpu.VMEM((d, tn), BF16)],
    )
    return pl.pallas_call(
        _moe_up_kernel,
        grid_spec=grid_spec,
        out_shape=jax.ShapeDtypeStruct((p, f), BF16),
        compiler_params=_params("arbitrary", "arbitrary"),
        name="moe_up",
    )(tile_expert, n_valid, xs, we_gate_up, we_gate_up)


def _moe_down_kernel(te_ref, nv_ref, h_ref, w_ref, y_ref, wb_ref):
    _load_weights(_new_expert(te_ref), (w_ref,), (wb_ref,))
    valid = pl.program_id(1) < nv_ref[0]

    @pl.when(valid)
    def _():
        y_ref[...] = jnp.dot(h_ref[...], wb_ref[...], preferred_element_type=F32).astype(y_ref.dtype)

    @pl.when(jnp.logical_not(valid))
    def _():
        y_ref[...] = jnp.zeros_like(y_ref)


def _moe_down(tile_expert, n_valid, hs, we_down, tm, tn):
    p, f = hs.shape
    d = we_down.shape[2]
    grid_spec = pltpu.PrefetchScalarGridSpec(
        num_scalar_prefetch=2,
        grid=(d // tn, p // tm),
        in_specs=[
            pl.BlockSpec((tm, f), lambda j, i, te, nv: (jnp.minimum(i, nv[0] - 1), 0)),
            pl.BlockSpec((1, f, tn), lambda j, i, te, nv: (te[i], 0, j)),
        ],
        out_specs=pl.BlockSpec((tm, tn), lambda j, i, te, nv: (i, j)),
        scratch_shapes=[pltpu.VMEM((f, tn), BF16)],
    )
    return pl.pallas_call(
        _moe_down_kernel,
        grid_spec=grid_spec,
        out_shape=jax.ShapeDtypeStruct((p, d), BF16),
        compiler_params=_params("arbitrary", "arbitrary"),
        name="moe_down",
    )(tile_expert, n_valid, hs, we_down)


def _moe_combine_kernel(x_ref, y1_ref, y2_ref, route_ref, g_ref, b_ref, o_ref,
                        acc_ref, mu_ref, rstd_ref, *, alpha):
    for s0 in range(0, acc_ref.shape[0], LN_SLAB):
        rs = slice(s0, s0 + LN_SLAB)
        y = (alpha * x_ref[rs, :].astype(F32) + route_ref[rs, 2:3] * y1_ref[rs, :].astype(F32)
             + route_ref[rs, 3:4] * y2_ref[rs, :].astype(F32))
        acc_ref[rs, :] = y
        mu_ref[rs, :] = jnp.mean(y, axis=-1, keepdims=True)
    _ln_finish(acc_ref, mu_ref, rstd_ref, o_ref, g_ref, b_ref)


def _moe_combine(xb, y1, y2, route, g, b, alpha, tm, out_dtype):
    t, d = xb.shape
    row = lambda i: (i, 0)
    kern = functools.partial(_moe_combine_kernel, alpha=alpha)
    return pl.pallas_call(
        kern,
        grid=(t // tm,),
        in_specs=[
            pl.BlockSpec((tm, d), row),
            pl.BlockSpec((tm, d), row),
            pl.BlockSpec((tm, d), row),
            pl.BlockSpec((tm, LANES), row),
            pl.BlockSpec((1, d), lambda i: (0, 0)),
            pl.BlockSpec((1, d), lambda i: (0, 0)),
        ],
        out_specs=pl.BlockSpec((tm, d), row),
        out_shape=jax.ShapeDtypeStruct((t, d), out_dtype),
        scratch_shapes=[pltpu.VMEM((tm, d), F32), pltpu.VMEM((tm, 1), F32), pltpu.VMEM((tm, 1), F32)],
        compiler_params=_params("parallel"),
        name="moe_combine_ln",
    )(xb, y1, y2, route, g.reshape(1, d), b.reshape(1, d))


def _route_plan(route, n_experts, tm):
    t = route.shape[0]
    e = route[:, :TOP_K].astype(jnp.int32).reshape(-1)
    onehot = (e[:, None] == jnp.arange(n_experts, dtype=jnp.int32)[None, :]).astype(jnp.int32)
    cum = jnp.cumsum(onehot, axis=0)
    rank = jnp.take_along_axis(cum, e[:, None], axis=1)[:, 0] - 1
    counts = cum[-1]
    padded = ((counts + tm - 1) // tm) * tm
    ends = jnp.cumsum(padded)
    pos = (ends - padded)[e] + rank
    p_rows = TOP_K * t + n_experts * tm
    row_token = jnp.zeros((p_rows,), jnp.int32).at[pos].set(jnp.arange(TOP_K * t, dtype=jnp.int32) // TOP_K)
    tile_start = jnp.arange(p_rows // tm, dtype=jnp.int32) * tm
    n_valid = (ends[-1] // tm).astype(jnp.int32).reshape(1)
    last_start = jnp.minimum(tile_start, ends[-1] - tm)
    tile_expert = jnp.minimum(jnp.searchsorted(ends, last_start, side="right"), n_experts - 1).astype(jnp.int32)
    return pos.reshape(t, TOP_K), row_token, tile_expert, n_valid


def _tile(dim, want):
    return min(dim, want)


def kernel(x, l0_w_in, l0_conv_w, l0_conv_b, l0_w_rgate, l0_b_rgate, l0_w_igate, l0_b_igate, l0_lru_lambda, l0_w_out, l0_ln1_g, l0_ln1_b, l0_w_gate_up, l0_w_down, l0_ln2_g, l0_ln2_b, l1_w_qkv, l1_b_qkv, l1_sinks, l1_w_o, l1_ln1_g, l1_ln1_b, l1_w_router, l1_b_router, l1_we_gate_up, l1_we_down, l1_ln2_g, l1_ln2_b):
    batch, seq, d = x.shape
    t = batch * seq
    depth = 2
    alpha = (2 * depth) ** 0.25
    n_q = l1_sinks.shape[0]
    n_kv = (l1_w_qkv.shape[1] // HEAD_DIM - n_q) // 2
    n_experts = l1_w_router.shape[1]

    x2d = x.reshape(t, d)
    xb = x2d.astype(BF16)
    bf = lambda w: w.astype(BF16)

    tm_big = _tile(t, 1024)
    tm_ln = _tile(t, 512)
    tn_ws = 256

    u, gg = _proj(xb, l0_w_in, tm_big, tn_ws)
    hg = _lru(u, gg, l0_conv_w, l0_conv_b, bf(l0_w_rgate), l0_b_rgate, bf(l0_w_igate), l0_b_igate,
              l0_lru_lambda, batch, _tile(seq, 256))
    x1 = _mm_ln(hg, bf(l0_w_out), x2d, l0_ln1_g, l0_ln1_b, alpha, tm_ln, _tile(d // 2, 1024))
    h = _swiglu_up(x1, l0_w_gate_up, tm_big, tn_ws)
    x2 = _mm_ln(h, bf(l0_w_down), x1, l0_ln2_g, l0_ln2_b, alpha, tm_ln, _tile(h.shape[1] // 2, 1024))

    pos = jnp.arange(seq, dtype=F32)
    inv = ROPE_THETA ** (-jnp.arange(0, HEAD_DIM, 2, dtype=F32) / HEAD_DIM)
    ang = pos[:, None] * inv[None, :]
    cos_f = jnp.concatenate([jnp.cos(ang), jnp.cos(ang)], axis=-1)
    sin_f = jnp.concatenate([-jnp.sin(ang), jnp.sin(ang)], axis=-1)
    qkv = _qkv(x2, l1_w_qkv, l1_b_qkv, cos_f, sin_f, (n_q + n_kv) * HEAD_DIM,
               _tile(seq, 1024), _tile(n_kv * HEAD_DIM, 512))
    o = _swa(qkv, l1_sinks, n_q, n_kv, seq, _tile(seq, 512))
    x3, route = _mm_ln(o, bf(l1_w_o), x2, l1_ln1_g, l1_ln1_b, alpha, tm_ln, _tile(d // 2, 1024),
                       router=(l1_w_router, l1_b_router))

    tm_e = _tile(t, 512)
    pos2, row_token, tile_expert, n_valid = _route_plan(route, n_experts, tm_e)
    xs = x3[row_token]
    hs = _moe_up(tile_expert, n_valid, xs, l1_we_gate_up, tm_e, _tile(d // 2, 512))
    ys = _moe_down(tile_expert, n_valid, hs, l1_we_down, tm_e, _tile(d // 2, 1024))
    out = _moe_combine(x3, ys[pos2[:, 0]], ys[pos2[:, 1]], route, l1_ln2_g, l1_ln2_b, alpha,
                       _tile(t, 256), x.dtype)
    return out.reshape(batch, seq, d)
```

```python
import functools

import jax
import jax.numpy as jnp
from jax import lax
from jax.experimental import pallas as pl
from jax.experimental.pallas import tpu as pltpu

F32 = jnp.float32
BF16 = jnp.bfloat16

HEAD_DIM = 128
WINDOW = 128
CONV_WIDTH = 4
LRU_C = 8.0
TOP_K = 2
ROPE_THETA = 10000.0
LN_EPS = 1e-5
LANES = 128
VMEM_LIMIT_BYTES = 56 * 1024 * 1024
LN_SLAB = 64
LN_CHUNK = 16


def _params(*sem):
    return pltpu.CompilerParams(dimension_semantics=sem, vmem_limit_bytes=VMEM_LIMIT_BYTES)


def _ln_finish(y_ref, mu_ref, rstd_ref, o_ref, g_ref, b_ref, per_chunk=None):
    rows = y_ref.shape[0]
    inv_n = 1.0 / y_ref.shape[1]
    for s0 in range(0, rows, LN_SLAB):
        rs = slice(s0, s0 + LN_SLAB)
        d = y_ref[rs, :] - mu_ref[rs, :]
        var = jnp.sum(d * d, axis=-1, keepdims=True) * inv_n
        rstd_ref[rs, :] = lax.rsqrt(var + LN_EPS)

    def body(c, carry):
        r0 = pl.multiple_of(c * LN_CHUNK, LN_CHUNK)
        rr = pl.ds(r0, LN_CHUNK)
        o = (y_ref[rr, :] - mu_ref[rr, :]) * rstd_ref[rr, :] * g_ref[...] + b_ref[...]
        o_ref[rr, :] = o.astype(o_ref.dtype)
        if per_chunk is not None:
            per_chunk(r0)
        return carry

    lax.fori_loop(0, rows // LN_CHUNK, body, 0, unroll=2)


def _load_weights(first, w_refs, wb_refs):
    @pl.when(first)
    def _():
        for w_ref, wb_ref in zip(w_refs, wb_refs):
            w = w_ref[0] if len(w_ref.shape) == 3 else w_ref[...]
            wb_ref[...] = w.astype(wb_ref.dtype)


def _proj_kernel(x_ref, wu_ref, wg_ref, u_ref, gg_ref, wub_ref, wgb_ref):
    _load_weights(pl.program_id(1) == 0, (wu_ref, wg_ref), (wub_ref, wgb_ref))
    x = x_ref[...]
    u_ref[...] = jnp.dot(x, wub_ref[...], preferred_element_type=F32)
    g = jnp.dot(x, wgb_ref[...], preferred_element_type=F32)
    gg_ref[...] = jax.nn.gelu(g, approximate=True).astype(gg_ref.dtype)


def _proj(xb, w_in, tm, tn):
    t, d = xb.shape
    w = w_in.shape[1] // 2
    nj = w // tn
    return pl.pallas_call(
        _proj_kernel,
        grid=(nj, t // tm),
        in_specs=[
            pl.BlockSpec((tm, d), lambda j, i: (i, 0)),
            pl.BlockSpec((d, tn), lambda j, i: (0, j)),
            pl.BlockSpec((d, tn), lambda j, i: (0, j + nj)),
        ],
        out_specs=[
            pl.BlockSpec((tm, tn), lambda j, i: (i, j)),
            pl.BlockSpec((tm, tn), lambda j, i: (i, j)),
        ],
        out_shape=[jax.ShapeDtypeStruct((t, w), F32), jax.ShapeDtypeStruct((t, w), BF16)],
        scratch_shapes=[pltpu.VMEM((d, tn), BF16), pltpu.VMEM((d, tn), BF16)],
        compiler_params=_params("arbitrary", "arbitrary"),
        name="l0_proj",
    )(xb, w_in, w_in)


def _swiglu_up_kernel(x_ref, wg_ref, wu_ref, h_ref, wgb_ref, wub_ref):
    _load_weights(pl.program_id(1) == 0, (wg_ref, wu_ref), (wgb_ref, wub_ref))
    x = x_ref[...]
    g = jnp.dot(x, wgb_ref[...], preferred_element_type=F32)
    u = jnp.dot(x, wub_ref[...], preferred_element_type=F32)
    h_ref[...] = (jax.nn.silu(g) * u).astype(h_ref.dtype)


def _swiglu_up(xb, w_gate_up, tm, tn):
    t, d = xb.shape
    f = w_gate_up.shape[1] // 2
    nj = f // tn
    return pl.pallas_call(
        _swiglu_up_kernel,
        grid=(nj, t // tm),
        in_specs=[
            pl.BlockSpec((tm, d), lambda j, i: (i, 0)),
            pl.BlockSpec((d, tn), lambda j, i: (0, j)),
            pl.BlockSpec((d, tn), lambda j, i: (0, j + nj)),
        ],
        out_specs=pl.BlockSpec((tm, tn), lambda j, i: (i, j)),
        out_shape=jax.ShapeDtypeStruct((t, f), BF16),
        scratch_shapes=[pltpu.VMEM((d, tn), BF16), pltpu.VMEM((d, tn), BF16)],
        compiler_params=_params("arbitrary", "arbitrary"),
        name="swiglu_up",
    )(xb, w_gate_up, w_gate_up)


def _qkv_kernel(x_ref, w_ref, b_ref, cos_ref, sin_ref, o_ref, wb_ref, *, n_rope_tiles):
    _load_weights(pl.program_id(1) == 0, (w_ref,), (wb_ref,))
    acc = jnp.dot(x_ref[...], wb_ref[...], preferred_element_type=F32) + b_ref[...]
    rope = pl.program_id(0) < n_rope_tiles
    cos = jnp.where(rope, cos_ref[...], 1.0)
    sin = jnp.where(rope, sin_ref[...], 0.0)
    for c in range(acc.shape[1] // HEAD_DIM):
        cs = slice(c * HEAD_DIM, (c + 1) * HEAD_DIM)
        tt = acc[:, cs]
        o_ref[:, cs] = (tt * cos + pltpu.roll(tt, HEAD_DIM // 2, 1) * sin).astype(o_ref.dtype)


def _qkv(xb, w_qkv, b_qkv, cos_f, sin_f, n_rope_cols, tm, tn):
    t, d = xb.shape
    n = w_qkv.shape[1]
    seq = cos_f.shape[0]
    n_pos_tiles = seq // tm
    kern = functools.partial(_qkv_kernel, n_rope_tiles=n_rope_cols // tn)
    return pl.pallas_call(
        kern,
        grid=(n // tn, t // tm),
        in_specs=[
            pl.BlockSpec((tm, d), lambda j, i: (i, 0)),
            pl.BlockSpec((d, tn), lambda j, i: (0, j)),
            pl.BlockSpec((1, tn), lambda j, i: (0, j)),
            pl.BlockSpec((tm, HEAD_DIM), lambda j, i: (i % n_pos_tiles, 0)),
            pl.BlockSpec((tm, HEAD_DIM), lambda j, i: (i % n_pos_tiles, 0)),
        ],
        out_specs=pl.BlockSpec((tm, tn), lambda j, i: (i, j)),
        out_shape=jax.ShapeDtypeStruct((t, n), BF16),
        scratch_shapes=[pltpu.VMEM((d, tn), BF16)],
        compiler_params=_params("arbitrary", "arbitrary"),
        name="l1_qkv_rope",
    )(xb, w_qkv, b_qkv.reshape(1, n), cos_f, sin_f)


def _lru_kernel(u_ref, gg_ref, cw_ref, cb_ref, wr_ref, br_ref, wi_ref, bi_ref, lam_ref,
                hg_ref, ext_ref, a_ref, b_ref, h_ref, *, ts, n_blocks, block_w, scan_w):
    halo = 8
    s = pl.program_id(1)

    @pl.when(s == 0)
    def _():
        ext_ref[0:halo, :] = jnp.zeros((halo, ext_ref.shape[1]), F32)
        h_ref[...] = jnp.zeros_like(h_ref)

    ext_ref[halo:halo + ts, :] = u_ref[...]

    for hd in range(n_blocks):
        cs = slice(hd * block_w, (hd + 1) * block_w)
        uc = cb_ref[:, cs]
        for k in range(CONV_WIDTH):
            off = halo - (CONV_WIDTH - 1) + k
            uc = uc + cw_ref[k:k + 1, cs] * ext_ref[off:off + ts, cs]
        ucb = uc.astype(BF16)
        r = jax.nn.sigmoid(jnp.dot(ucb, wr_ref[hd], preferred_element_type=F32) + br_ref[:, cs])
        ig = jax.nn.sigmoid(jnp.dot(ucb, wi_ref[hd], preferred_element_type=F32) + bi_ref[:, cs])
        log_a = -LRU_C * r * jax.nn.softplus(-lam_ref[:, cs])
        a = jnp.exp(log_a)
        a_ref[:, cs] = a
        b_ref[:, cs] = jnp.sqrt(-jnp.tanh(log_a) * (a * a + 1.0)) * (ig * uc)

    ext_ref[0:halo, :] = ext_ref[ts:ts + halo, :]

    width = a_ref.shape[1]
    sub = 8
    row = lax.broadcasted_iota(jnp.int32, (sub, scan_w), 0)
    for c in range(width // scan_w):
        cs = pl.ds(c * scan_w, scan_w)

        def group(gi, h, cs=cs):
            rr = pl.ds(pl.multiple_of(gi * sub, sub), sub)
            a = a_ref[rr, cs]
            b = b_ref[rr, cs]
            for dist in (1, 2, 4):
                keep = row >= dist
                b = b + a * jnp.where(keep, pltpu.roll(b, dist, 0), 0.0)
                a = a * jnp.where(keep, pltpu.roll(a, dist, 0), 1.0)
            hh = a * h + b
            b_ref[rr, cs] = hh
            return hh[sub - 1:sub, :]

        h_ref[0:1, cs] = lax.fori_loop(0, ts // sub, group, h_ref[0:1, cs], unroll=2)

    hg_ref[...] = (b_ref[...] * gg_ref[...].astype(F32)).astype(hg_ref.dtype)


def _lru(u, gg, conv_w, conv_b, w_r, b_r, w_i, b_i, lam, batch, ts):
    t, w = u.shape
    n_blocks, block_w, _ = w_r.shape
    ns = t // batch // ts
    row = lambda b, s: (b * ns + s, 0)
    full2 = lambda b, s: (0, 0)
    full3 = lambda b, s: (0, 0, 0)
    kern = functools.partial(_lru_kernel, ts=ts, n_blocks=n_blocks, block_w=block_w,
                             scan_w=min(w, 1024))
    return pl.pallas_call(
        kern,
        grid=(batch, ns),
        in_specs=[
            pl.BlockSpec((ts, w), row),
            pl.BlockSpec((ts, w), row),
            pl.BlockSpec((CONV_WIDTH, w), full2),
            pl.BlockSpec((1, w), full2),
            pl.BlockSpec((n_blocks, block_w, block_w), full3),
            pl.BlockSpec((1, w), full2),
            pl.BlockSpec((n_blocks, block_w, block_w), full3),
            pl.BlockSpec((1, w), full2),
            pl.BlockSpec((1, w), full2),
        ],
        out_specs=pl.BlockSpec((ts, w), row),
        out_shape=jax.ShapeDtypeStruct((t, w), BF16),
        scratch_shapes=[
            pltpu.VMEM((ts + 8, w), F32),
            pltpu.VMEM((ts, w), F32),
            pltpu.VMEM((ts, w), F32),
            pltpu.VMEM((8, w), F32),
        ],
        compiler_params=_params("arbitrary", "arbitrary"),
        name="l0_rglru",
    )(u, gg, conv_w, conv_b.reshape(1, w), w_r, b_r.reshape(1, w), w_i, b_i.reshape(1, w),
      lam.reshape(1, w))


def _mm_ln_kernel(a_ref, w_ref, r_ref, g_ref, b_ref, *rest, nk, alpha, n_experts):
    if n_experts:
        wr_ref, br_ref, o_ref, route_ref, acc_ref, mu_ref, rstd_ref = rest
    else:
        o_ref, acc_ref, mu_ref, rstd_ref = rest
    k = pl.program_id(1)

    @pl.when(k == 0)
    def _():
        acc_ref[...] = jnp.dot(a_ref[...], w_ref[...], preferred_element_type=F32)

    @pl.when(jnp.logical_and(k > 0, k < nk - 1))
    def _():
        acc_ref[...] += jnp.dot(a_ref[...], w_ref[...], preferred_element_type=F32)

    @pl.when(k == nk - 1)
    def _():
        y = (acc_ref[...] + jnp.dot(a_ref[...], w_ref[...], preferred_element_type=F32)
             + alpha * r_ref[...].astype(F32))
        acc_ref[...] = y
        mu_ref[...] = jnp.mean(y, axis=-1, keepdims=True)
        _ln_finish(acc_ref, mu_ref, rstd_ref, o_ref, g_ref, b_ref)
        if n_experts:
            logits = jnp.dot(o_ref[...].astype(BF16), wr_ref[...],
                             preferred_element_type=F32) + br_ref[...]
            route_ref[...] = _top2_route(logits, n_experts)


def _top2_route(logits, n_experts):
    lane = lax.broadcasted_iota(jnp.int32, logits.shape, 1)
    neg = jnp.float32(-jnp.inf)
    lg = jnp.where(lane < n_experts, logits, neg)
    m1 = jnp.max(lg, axis=-1, keepdims=True)
    i1 = jnp.min(jnp.where(lg == m1, lane, LANES), axis=-1, keepdims=True)
    lg2 = jnp.where(lane == i1, neg, lg)
    m2 = jnp.max(lg2, axis=-1, keepdims=True)
    i2 = jnp.min(jnp.where(lg2 == m2, lane, LANES), axis=-1, keepdims=True)
    e = jnp.exp(m2 - m1)
    w1 = 1.0 / (1.0 + e)
    w2 = e / (1.0 + e)
    out = jnp.where(lane == 0, i1.astype(F32), 0.0)
    out = jnp.where(lane == 1, i2.astype(F32), out)
    out = jnp.where(lane == 2, w1, out)
    out = jnp.where(lane == 3, w2, out)
    return out


def _mm_ln(a, w, resid, g, b, alpha, tm, tk, router=None, out_dtype=BF16):
    t, kdim = a.shape
    n = w.shape[1]
    nk = kdim // tk
    assert nk >= 2
    n_experts = 0
    in_specs = [
        pl.BlockSpec((tm, tk), lambda i, k: (i, k)),
        pl.BlockSpec((tk, n), lambda i, k: (k, 0)),
        pl.BlockSpec((tm, n), lambda i, k: (i, 0)),
        pl.BlockSpec((1, n), lambda i, k: (0, 0)),
        pl.BlockSpec((1, n), lambda i, k: (0, 0)),
    ]
    args = [a, w, resid, g.reshape(1, n), b.reshape(1, n)]
    out_specs = [pl.BlockSpec((tm, n), lambda i, k: (i, 0))]
    out_shape = [jax.ShapeDtypeStruct((t, n), out_dtype)]
    if router is not None:
        w_router, b_router = router
        n_experts = w_router.shape[1]
        wr = jnp.zeros((n, LANES), BF16).at[:, :n_experts].set(w_router.astype(BF16))
        br = jnp.zeros((1, LANES), F32).at[0, :n_experts].set(b_router.astype(F32))
        in_specs += [pl.BlockSpec((n, LANES), lambda i, k: (0, 0)),
                     pl.BlockSpec((1, LANES), lambda i, k: (0, 0))]
        args += [wr, br]
        out_specs.append(pl.BlockSpec((tm, LANES), lambda i, k: (i, 0)))
        out_shape.append(jax.ShapeDtypeStruct((t, LANES), F32))
    kern = functools.partial(_mm_ln_kernel, nk=nk, alpha=alpha, n_experts=n_experts)
    out = pl.pallas_call(
        kern,
        grid=(t // tm, nk),
        in_specs=in_specs,
        out_specs=out_specs,
        out_shape=out_shape,
        scratch_shapes=[pltpu.VMEM((tm, n), F32), pltpu.VMEM((tm, 1), F32), pltpu.VMEM((tm, 1), F32)],
        compiler_params=_params("parallel", "arbitrary"),
        name="mm_res_ln",
    )(*args)
    return out if router is not None else out[0]


def _swa_kernel(sink_ref, q_ref, k_ref, v_ref, kp_ref, vp_ref, o_ref, *, rows, n_kv, group,
                tiles_per_seq):
    i = pl.program_id(0)
    nqb = rows // WINDOW
    gq = group * WINDOW
    scale = HEAD_DIM ** -0.5
    neg = jnp.float32(-jnp.inf)

    qi = lax.broadcasted_iota(jnp.int32, (gq, 2 * WINDOW), 0) % WINDOW
    kj = lax.broadcasted_iota(jnp.int32, (gq, 2 * WINDOW), 1)
    delta = qi + WINDOW - kj
    band = (delta >= 0) & (delta < WINDOW)
    bias = jnp.where(band, 0.0, neg)
    seq_start = (i % tiles_per_seq) == 0
    bias_first = jnp.where(jnp.logical_and(seq_start, kj < WINDOW), neg, bias)
    grp = lax.broadcasted_iota(jnp.int32, (gq, 1), 0) // WINDOW

    for hk in range(n_kv):
        ks = slice(hk * HEAD_DIM, (hk + 1) * HEAD_DIM)
        sink = jnp.zeros((gq, 1), F32)
        for g in range(group):
            sink = jnp.where(grp == g, sink_ref[hk * group + g], sink)
        for qb in range(nqb):
            rs = slice(qb * WINDOW, (qb + 1) * WINDOW)
            if qb == 0:
                k_prev, v_prev = kp_ref[:, ks], vp_ref[:, ks]
            else:
                ps = slice((qb - 1) * WINDOW, qb * WINDOW)
                k_prev, v_prev = k_ref[ps, ks], v_ref[ps, ks]
            k_win = jnp.concatenate([k_prev, k_ref[rs, ks]], axis=0)
            v_win = jnp.concatenate([v_prev, v_ref[rs, ks]], axis=0)
            q4 = jnp.concatenate(
                [q_ref[rs, (hk * group + g) * HEAD_DIM:(hk * group + g + 1) * HEAD_DIM]
                 for g in range(group)], axis=0)
            sc = lax.dot_general(q4, k_win, (((1,), (1,)), ((), ())),
                                 preferred_element_type=F32) * scale
            sc = sc + (bias_first if qb == 0 else bias)
            m = jnp.maximum(jnp.max(sc, axis=-1, keepdims=True), sink)
            p = jnp.exp(sc - m)
            denom = jnp.sum(p, axis=-1, keepdims=True) + jnp.exp(sink - m)
            o = jnp.dot((p / denom).astype(v_win.dtype), v_win, preferred_element_type=F32)
            for g in range(group):
                hs = slice((hk * group + g) * HEAD_DIM, (hk * group + g + 1) * HEAD_DIM)
                o_ref[rs, hs] = o[g * WINDOW:(g + 1) * WINDOW, :].astype(o_ref.dtype)


def _swa(qkv, sinks, n_q, n_kv, seq, rows):
    t = qkv.shape[0]
    dq = n_q * HEAD_DIM
    dkv = n_kv * HEAD_DIM
    assert dq % dkv == 0
    k_blk = dq // dkv
    nqb = rows // WINDOW
    kern = functools.partial(_swa_kernel, rows=rows, n_kv=n_kv, group=n_q // n_kv,
                             tiles_per_seq=seq // rows)
    prev = lambda c: (lambda i, s: (jnp.maximum(i * nqb - 1, 0), c))
    grid_spec = pltpu.PrefetchScalarGridSpec(
        num_scalar_prefetch=1,
        grid=(t // rows,),
        in_specs=[
            pl.BlockSpec((rows, dq), lambda i, s: (i, 0)),
            pl.BlockSpec((rows, dkv), lambda i, s: (i, k_blk)),
            pl.BlockSpec((rows, dkv), lambda i, s: (i, k_blk + 1)),
            pl.BlockSpec((WINDOW, dkv), prev(k_blk)),
            pl.BlockSpec((WINDOW, dkv), prev(k_blk + 1)),
        ],
        out_specs=pl.BlockSpec((rows, dq), lambda i, s: (i, 0)),
    )
    return pl.pallas_call(
        kern,
        grid_spec=grid_spec,
        out_shape=jax.ShapeDtypeStruct((t, dq), BF16),
        compiler_params=_params("arbitrary"),
        name="l1_swa",
    )(sinks.astype(F32), qkv, qkv, qkv, qkv, qkv)


def _cast_kernel(x_ref, o_ref):
    o_ref[...] = x_ref[...].astype(o_ref.dtype)


def _cast_bf16(x, tm):
    r, c = x.shape
    return pl.pallas_call(
        _cast_kernel,
        grid=(r // tm,),
        in_specs=[pl.BlockSpec((tm, c), lambda i: (i, 0))],
        out_specs=pl.BlockSpec((tm, c), lambda i: (i, 0)),
        out_shape=jax.ShapeDtypeStruct((r, c), BF16),
        compiler_params=_params("parallel"),
        name="cast_bf16",
    )(x)


def _row_copy(src_ref, idx, dst_ref, r, sem):
    return pltpu.make_async_copy(src_ref.at[pl.ds(idx, 1), :], dst_ref.at[pl.ds(r, 1), :], sem)


def _gather_rows_kernel(idx_ref, src_ref, o_ref, buf_ref, sem):
    rows = buf_ref.shape[0]

    def start(r, c):
        _row_copy(src_ref, idx_ref[0, 0, r], buf_ref, r, sem).start()
        return c

    lax.fori_loop(0, rows, start, 0, unroll=8)

    def wait(r, c):
        _row_copy(src_ref, 0, buf_ref, r, sem).wait()
        return c

    lax.fori_loop(0, rows, wait, 0, unroll=8)
    o_ref[...] = buf_ref[...].astype(o_ref.dtype)


def _gather_rows(src, idx, tm, out_dtype):
    p = idx.shape[0]
    d = src.shape[1]
    return pl.pallas_call(
        _gather_rows_kernel,
        grid=(p // tm,),
        in_specs=[
            pl.BlockSpec((1, 1, tm), lambda i: (i, 0, 0), memory_space=pltpu.SMEM),
            pl.BlockSpec(memory_space=pl.ANY),
        ],
        out_specs=pl.BlockSpec((tm, d), lambda i: (i, 0)),
        out_shape=jax.ShapeDtypeStruct((p, d), out_dtype),
        scratch_shapes=[pltpu.VMEM((tm, d), src.dtype), pltpu.SemaphoreType.DMA(())],
        compiler_params=_params("arbitrary"),
        name="moe_gather_rows",
    )(idx.reshape(p // tm, 1, tm), src)


def _new_expert(te_ref):
    i = pl.program_id(1)
    return jnp.logical_or(i == 0, te_ref[i] != te_ref[jnp.maximum(i - 1, 0)])


def _moe_up_kernel(te_ref, nv_ref, x_ref, wg_ref, wu_ref, h_ref, wgb_ref, wub_ref):
    _load_weights(_new_expert(te_ref), (wg_ref, wu_ref), (wgb_ref, wub_ref))
    valid = pl.program_id(1) < nv_ref[0]

    @pl.when(valid)
    def _():
        x = x_ref[...]
        g = jnp.dot(x, wgb_ref[...], preferred_element_type=F32)
        u = jnp.dot(x, wub_ref[...], preferred_element_type=F32)
        h_ref[...] = (jax.nn.silu(g) * u).astype(h_ref.dtype)

    @pl.when(jnp.logical_not(valid))
    def _():
        h_ref[...] = jnp.zeros_like(h_ref)


def _moe_up(tile_expert, n_valid, xs, we_gate_up, tm, tn):
    p, d = xs.shape
    f = we_gate_up.shape[2] // 2
    nj = f // tn
    row = lambda j, i, te, nv: (jnp.minimum(i, nv[0] - 1), 0)
    grid_spec = pltpu.PrefetchScalarGridSpec(
        num_scalar_prefetch=2,
        grid=(nj, p // tm),
        in_specs=[
            pl.BlockSpec((tm, d), row),
            pl.BlockSpec((1, d, tn), lambda j, i, te, nv: (te[i], 0, j)),
            pl.BlockSpec((1, d, tn), lambda j, i, te, nv: (te[i], 0, j + nj)),
        ],
        out_specs=pl.BlockSpec((tm, tn), lambda j, i, te, nv: (i, j)),
        scratch_shapes=[pltpu.VMEM((d, tn), BF16), pltpu.VMEM((d, tn), BF16)],
    )
    return pl.pallas_call(
        _moe_up_kernel,
        grid_spec=grid_spec,
        out_shape=jax.ShapeDtypeStruct((p, f), BF16),
        compiler_params=_params("arbitrary", "arbitrary"),
        name="moe_up",
    )(tile_expert, n_valid, xs, we_gate_up, we_gate_up)


def _moe_down_kernel(te_ref, nv_ref, h_ref, w_ref, y_ref, wb_ref):
    _load_weights(_new_expert(te_ref), (w_ref,), (wb_ref,))
    valid = pl.program_id(1) < nv_ref[0]

    @pl.when(valid)
    def _():
        y_ref[...] = jnp.dot(h_ref[...], wb_ref[...], preferred_element_type=F32).astype(y_ref.dtype)

    @pl.when(jnp.logical_not(valid))
    def _():
        y_ref[...] = jnp.zeros_like(y_ref)


def _moe_down(tile_expert, n_valid, hs, we_down, tm, tn):
    p, f = hs.shape
    d = we_down.shape[2]
    grid_spec = pltpu.PrefetchScalarGridSpec(
        num_scalar_prefetch=2,
        grid=(d // tn, p // tm),
        in_specs=[
            pl.BlockSpec((tm, f), lambda j, i, te, nv: (jnp.minimum(i, nv[0] - 1), 0)),
            pl.BlockSpec((1, f, tn), lambda j, i, te, nv: (te[i], 0, j)),
        ],
        out_specs=pl.BlockSpec((tm, tn), lambda j, i, te, nv: (i, j)),
        scratch_shapes=[pltpu.VMEM((f, tn), BF16)],
    )
    return pl.pallas_call(
        _moe_down_kernel,
        grid_spec=grid_spec,
        out_shape=jax.ShapeDtypeStruct((p, d), F32),
        compiler_params=_params("arbitrary", "arbitrary"),
        name="moe_down",
    )(tile_expert, n_valid, hs, we_down)


def _moe_combine_kernel(pos_ref, pos_next_ref, x_ref, ys_ref, route_ref, g_ref, b_ref, o_ref,
                        ybuf_ref, acc_ref, mu_ref, rstd_ref, sem, *, alpha):
    i = pl.program_id(0)
    n = pl.num_programs(0)
    rows = acc_ref.shape[0]
    slot = i % 2

    def start_rows(p_ref, s, r0, count):
        for r in range(count):
            for k in range(TOP_K):
                _row_copy(ys_ref, p_ref[0, 0, TOP_K * (r0 + r) + k], ybuf_ref.at[s, k], r0 + r,
                          sem.at[s]).start()

    def wait_tile(s):
        def body(r, c):
            for k in range(TOP_K):
                _row_copy(ys_ref, 0, ybuf_ref.at[s, k], r, sem.at[s]).wait()
            return c
        lax.fori_loop(0, rows, body, 0, unroll=4)

    @pl.when(i == 0)
    def _():
        lax.fori_loop(0, rows // LN_CHUNK,
                      lambda c, carry: (start_rows(pos_ref, 0, c * LN_CHUNK, LN_CHUNK), carry)[1], 0)

    wait_tile(slot)

    for s0 in range(0, rows, LN_SLAB):
        rs = slice(s0, s0 + LN_SLAB)
        y = (alpha * x_ref[rs, :].astype(F32) + route_ref[rs, 2:3] * ybuf_ref[slot, 0, rs, :]
             + route_ref[rs, 3:4] * ybuf_ref[slot, 1, rs, :])
        acc_ref[rs, :] = y
        mu_ref[rs, :] = jnp.mean(y, axis=-1, keepdims=True)

    _ln_finish(acc_ref, mu_ref, rstd_ref, o_ref, g_ref, b_ref,
               per_chunk=lambda r0: start_rows(pos_next_ref, 1 - slot, r0, LN_CHUNK))

    @pl.when(i == n - 1)
    def _():
        wait_tile(1 - slot)


def _moe_combine(x, ys, pos, route, g, b, alpha, tm, out_dtype):
    t, d = x.shape
    nt = t // tm
    row = lambda i: (i, 0)
    pos3 = pos.reshape(nt, 1, TOP_K * tm)
    kern = functools.partial(_moe_combine_kernel, alpha=alpha)
    return pl.pallas_call(
        kern,
        grid=(nt,),
        in_specs=[
            pl.BlockSpec((1, 1, TOP_K * tm), lambda i: (i, 0, 0), memory_space=pltpu.SMEM),
            pl.BlockSpec((1, 1, TOP_K * tm), lambda i: (jnp.minimum(i + 1, nt - 1), 0, 0),
                         memory_space=pltpu.SMEM),
            pl.BlockSpec((tm, d), row),
            pl.BlockSpec(memory_space=pl.ANY),
            pl.BlockSpec((tm, LANES), row),
            pl.BlockSpec((1, d), lambda i: (0, 0)),
            pl.BlockSpec((1, d), lambda i: (0, 0)),
        ],
        out_specs=pl.BlockSpec((tm, d), row),
        out_shape=jax.ShapeDtypeStruct((t, d), out_dtype),
        scratch_shapes=[
            pltpu.VMEM((2, TOP_K, tm, d), ys.dtype),
            pltpu.VMEM((tm, d), F32),
            pltpu.VMEM((tm, 1), F32),
            pltpu.VMEM((tm, 1), F32),
            pltpu.SemaphoreType.DMA((2,)),
        ],
        compiler_params=_params("arbitrary"),
        name="moe_combine_ln",
    )(pos3, pos3, x, ys, route, g.reshape(1, d), b.reshape(1, d))


def _route_plan(route, n_experts, tm):
    t = route.shape[0]
    e = route[:, :TOP_K].astype(jnp.int32).reshape(-1)
    onehot = (e[:, None] == jnp.arange(n_experts, dtype=jnp.int32)[None, :]).astype(jnp.int32)
    cum = jnp.cumsum(onehot, axis=0)
    rank = jnp.take_along_axis(cum, e[:, None], axis=1)[:, 0] - 1
    counts = cum[-1]
    padded = ((counts + tm - 1) // tm) * tm
    ends = jnp.cumsum(padded)
    pos = (ends - padded)[e] + rank
    p_rows = TOP_K * t + n_experts * tm
    row_token = jnp.zeros((p_rows,), jnp.int32).at[pos].set(jnp.arange(TOP_K * t, dtype=jnp.int32) // TOP_K)
    tile_start = jnp.arange(p_rows // tm, dtype=jnp.int32) * tm
    n_valid = (ends[-1] // tm).astype(jnp.int32).reshape(1)
    last_start = jnp.minimum(tile_start, ends[-1] - tm)
    tile_expert = jnp.minimum(jnp.searchsorted(ends, last_start, side="right"), n_experts - 1).astype(jnp.int32)
    return pos.reshape(t, TOP_K), row_token, tile_expert, n_valid


def _tile(dim, want):
    return min(dim, want)


def kernel(x, l0_w_in, l0_conv_w, l0_conv_b, l0_w_rgate, l0_b_rgate, l0_w_igate, l0_b_igate, l0_lru_lambda, l0_w_out, l0_ln1_g, l0_ln1_b, l0_w_gate_up, l0_w_down, l0_ln2_g, l0_ln2_b, l1_w_qkv, l1_b_qkv, l1_sinks, l1_w_o, l1_ln1_g, l1_ln1_b, l1_w_router, l1_b_router, l1_we_gate_up, l1_we_down, l1_ln2_g, l1_ln2_b):
    batch, seq, d = x.shape
    t = batch * seq
    depth = 2
    alpha = (2 * depth) ** 0.25
    n_q = l1_sinks.shape[0]
    n_kv = (l1_w_qkv.shape[1] // HEAD_DIM - n_q) // 2
    n_experts = l1_w_router.shape[1]

    x2d = x.reshape(t, d)
    tm_cast = _tile(d, 512)
    xb = _cast_bf16(x2d, tm_cast)
    bf = lambda w: w.astype(BF16)
    w_out_b = _cast_bf16(l0_w_out, tm_cast)
    w_down_b = _cast_bf16(l0_w_down, tm_cast)
    w_o_b = _cast_bf16(l1_w_o, tm_cast)

    tm_big = _tile(t, 1024)
    tm_ln = _tile(t, 512)
    tn_ws = 256

    u, gg = _proj(xb, l0_w_in, tm_big, tn_ws)
    hg = _lru(u, gg, l0_conv_w, l0_conv_b, bf(l0_w_rgate), l0_b_rgate, bf(l0_w_igate), l0_b_igate,
              l0_lru_lambda, batch, _tile(seq, 256))
    x1 = _mm_ln(hg, w_out_b, x2d, l0_ln1_g, l0_ln1_b, alpha, tm_ln, _tile(d // 2, 1024))
    h = _swiglu_up(x1, l0_w_gate_up, tm_big, tn_ws)
    x2 = _mm_ln(h, w_down_b, x1, l0_ln2_g, l0_ln2_b, alpha, tm_ln, _tile(h.shape[1] // 2, 1024))

    pos = jnp.arange(seq, dtype=F32)
    inv = ROPE_THETA ** (-jnp.arange(0, HEAD_DIM, 2, dtype=F32) / HEAD_DIM)
    ang = pos[:, None] * inv[None, :]
    cos_f = jnp.concatenate([jnp.cos(ang), jnp.cos(ang)], axis=-1)
    sin_f = jnp.concatenate([-jnp.sin(ang), jnp.sin(ang)], axis=-1)
    qkv = _qkv(x2, l1_w_qkv, l1_b_qkv, cos_f, sin_f, (n_q + n_kv) * HEAD_DIM,
               _tile(seq, 1024), _tile(n_kv * HEAD_DIM, 512))
    o = _swa(qkv, l1_sinks, n_q, n_kv, seq, _tile(seq, 512))
    x3, route = _mm_ln(o, w_o_b, x2, l1_ln1_g, l1_ln1_b, alpha, tm_ln, _tile(d // 2, 1024),
                       router=(l1_w_router, l1_b_router), out_dtype=F32)

    tm_e = _tile(t, 512)
    pos2, row_token, tile_expert, n_valid = _route_plan(route, n_experts, tm_e)
    xs = _gather_rows(x3, row_token, tm_e, BF16)
    hs = _moe_up(tile_expert, n_valid, xs, l1_we_gate_up, tm_e, _tile(d // 2, 512))
    ys = _moe_down(tile_expert, n_valid, hs, l1_we_down, tm_e, _tile(d // 2, 1024))
    out = _moe_combine(x3, ys, pos2, route, l1_ln2_g, l1_ln2_b, alpha, _tile(t, 256), x.dtype)
    return out.reshape(batch, seq, d)
```

```python
import functools

import jax
import jax.numpy as jnp
from jax import lax
from jax.experimental import pallas as pl
from jax.experimental.pallas import tpu as pltpu

F32 = jnp.float32
BF16 = jnp.bfloat16

HEAD_DIM = 128
WINDOW = 128
CONV_WIDTH = 4
LRU_C = 8.0
TOP_K = 2
ROPE_THETA = 10000.0
LN_EPS = 1e-5
LANES = 128
VMEM_LIMIT_BYTES = 56 * 1024 * 1024
LN_SLAB = 64
LN_CHUNK = 16


def _params(*sem):
    return pltpu.CompilerParams(dimension_semantics=sem, vmem_limit_bytes=VMEM_LIMIT_BYTES)


def _ln_finish(y_ref, mu_ref, rstd_ref, o_ref, g_ref, b_ref, per_chunk=None):
    rows = y_ref.shape[0]
    inv_n = 1.0 / y_ref.shape[1]
    for s0 in range(0, rows, LN_SLAB):
        rs = slice(s0, s0 + LN_SLAB)
        d = y_ref[rs, :] - mu_ref[rs, :]
        var = jnp.sum(d * d, axis=-1, keepdims=True) * inv_n
        rstd_ref[rs, :] = lax.rsqrt(var + LN_EPS)

    def body(c, carry):
        r0 = pl.multiple_of(c * LN_CHUNK, LN_CHUNK)
        rr = pl.ds(r0, LN_CHUNK)
        o = (y_ref[rr, :] - mu_ref[rr, :]) * rstd_ref[rr, :] * g_ref[...] + b_ref[...]
        o_ref[rr, :] = o.astype(o_ref.dtype)
        if per_chunk is not None:
            per_chunk(r0)
        return carry

    lax.fori_loop(0, rows // LN_CHUNK, body, 0, unroll=2)


def _load_weights(first, w_refs, wb_refs):
    @pl.when(first)
    def _():
        for w_ref, wb_ref in zip(w_refs, wb_refs):
            w = w_ref[0] if len(w_ref.shape) == 3 else w_ref[...]
            wb_ref[...] = w.astype(wb_ref.dtype)


def _pair_kernel(x_ref, w_hbm, *rest, epilogue, n_out, nj, ni):
    outs = rest[:n_out]
    wb_ref, stage_ref, sem = rest[n_out:]
    j = pl.program_id(0)
    i = pl.program_id(1)
    ks = stage_ref.shape[2]
    tn = stage_ref.shape[3]
    cur = j % 2
    nxt = 1 - cur

    def slab_copy(jj, s, m, st):
        col = pl.multiple_of((jj + m * nj) * tn, tn)
        row = pl.multiple_of(s * ks, ks)
        return pltpu.make_async_copy(w_hbm.at[pl.ds(row, ks), pl.ds(col, tn)], stage_ref.at[st, m],
                                     sem.at[st, m])

    def land(jj, s, slot_w):
        st = s % 2
        for m in range(2):
            slab_copy(jj, s, m, st).wait()
            wb_ref[slot_w, m, pl.ds(pl.multiple_of(s * ks, ks), ks), :] = (
                stage_ref[st, m].astype(wb_ref.dtype))

    @pl.when(jnp.logical_and(j == 0, i == 0))
    def _():
        def body(s, c):
            for m in range(2):
                slab_copy(0, s, m, s % 2).start()
            land(0, s, 0)
            return c
        lax.fori_loop(0, ni, body, 0)

    has_next = j + 1 < nj

    @pl.when(has_next)
    def _():
        for m in range(2):
            slab_copy(j + 1, i, m, i % 2).start()

    @pl.when(jnp.logical_and(has_next, i > 0))
    def _():
        land(j + 1, i - 1, nxt)

    x = x_ref[...]
    y0 = jnp.dot(x, wb_ref[cur, 0], preferred_element_type=F32)
    y1 = jnp.dot(x, wb_ref[cur, 1], preferred_element_type=F32)
    epilogue(y0, y1, *outs)

    @pl.when(jnp.logical_and(has_next, i == ni - 1))
    def _():
        land(j + 1, ni - 1, nxt)


def _pair_matmul(xb, w, tm, tn, epilogue, out_dtypes, name):
    t, d = xb.shape
    f = w.shape[1] // 2
    nj = f // tn
    ni = t // tm
    ks = d // ni
    assert ks * ni == d and ks % 16 == 0
    kern = functools.partial(_pair_kernel, epilogue=epilogue, n_out=len(out_dtypes), nj=nj, ni=ni)
    return pl.pallas_call(
        kern,
        grid=(nj, ni),
        in_specs=[
            pl.BlockSpec((tm, d), lambda j, i: (i, 0)),
            pl.BlockSpec(memory_space=pl.ANY),
        ],
        out_specs=[pl.BlockSpec((tm, tn), lambda j, i: (i, j)) for _ in out_dtypes],
        out_shape=[jax.ShapeDtypeStruct((t, f), dt) for dt in out_dtypes],
        scratch_shapes=[
            pltpu.VMEM((2, 2, d, tn), BF16),
            pltpu.VMEM((2, 2, ks, tn), w.dtype),
            pltpu.SemaphoreType.DMA((2, 2)),
        ],
        compiler_params=_params("arbitrary", "arbitrary"),
        name=name,
    )(xb, w)


def _proj_epilogue(u, g, u_ref, gg_ref):
    u_ref[...] = u
    gg_ref[...] = jax.nn.gelu(g, approximate=True).astype(gg_ref.dtype)


def _swiglu_epilogue(g, u, h_ref):
    h_ref[...] = (jax.nn.silu(g) * u).astype(h_ref.dtype)


def _qkv_kernel(x_ref, w_ref, b_ref, cos_ref, sin_ref, o_ref, wb_ref, *, n_rope_tiles):
    _load_weights(pl.program_id(1) == 0, (w_ref,), (wb_ref,))
    acc = jnp.dot(x_ref[...], wb_ref[...], preferred_element_type=F32) + b_ref[...]
    rope = pl.program_id(0) < n_rope_tiles
    cos = jnp.where(rope, cos_ref[...], 1.0)
    sin = jnp.where(rope, sin_ref[...], 0.0)
    for c in range(acc.shape[1] // HEAD_DIM):
        cs = slice(c * HEAD_DIM, (c + 1) * HEAD_DIM)
        tt = acc[:, cs]
        o_ref[:, cs] = (tt * cos + pltpu.roll(tt, HEAD_DIM // 2, 1) * sin).astype(o_ref.dtype)


def _qkv(xb, w_qkv, b_qkv, cos_f, sin_f, n_rope_cols, tm, tn):
    t, d = xb.shape
    n = w_qkv.shape[1]
    seq = cos_f.shape[0]
    n_pos_tiles = seq // tm
    kern = functools.partial(_qkv_kernel, n_rope_tiles=n_rope_cols // tn)
    return pl.pallas_call(
        kern,
        grid=(n // tn, t // tm),
        in_specs=[
            pl.BlockSpec((tm, d), lambda j, i: (i, 0)),
            pl.BlockSpec((d, tn), lambda j, i: (0, j)),
            pl.BlockSpec((1, tn), lambda j, i: (0, j)),
            pl.BlockSpec((tm, HEAD_DIM), lambda j, i: (i % n_pos_tiles, 0)),
            pl.BlockSpec((tm, HEAD_DIM), lambda j, i: (i % n_pos_tiles, 0)),
        ],
        out_specs=pl.BlockSpec((tm, tn), lambda j, i: (i, j)),
        out_shape=jax.ShapeDtypeStruct((t, n), BF16),
        scratch_shapes=[pltpu.VMEM((d, tn), BF16)],
        compiler_params=_params("arbitrary", "arbitrary"),
        name="l1_qkv_rope",
    )(xb, w_qkv, b_qkv.reshape(1, n), cos_f, sin_f)


def _lru_kernel(u_ref, gg_ref, cw_ref, cb_ref, wr_ref, br_ref, wi_ref, bi_ref, lam_ref,
                hg_ref, ext_ref, a_ref, b_ref, h_ref, *, ts, n_blocks, block_w, scan_w):
    halo = 8
    s = pl.program_id(1)

    @pl.when(s == 0)
    def _():
        ext_ref[0:halo, :] = jnp.zeros((halo, ext_ref.shape[1]), F32)
        h_ref[...] = jnp.zeros_like(h_ref)

    ext_ref[halo:halo + ts, :] = u_ref[...]

    for hd in range(n_blocks):
        cs = slice(hd * block_w, (hd + 1) * block_w)
        uc = cb_ref[:, cs]
        for k in range(CONV_WIDTH):
            off = halo - (CONV_WIDTH - 1) + k
            uc = uc + cw_ref[k:k + 1, cs] * ext_ref[off:off + ts, cs]
        ucb = uc.astype(BF16)
        r = jax.nn.sigmoid(jnp.dot(ucb, wr_ref[hd], preferred_element_type=F32) + br_ref[:, cs])
        ig = jax.nn.sigmoid(jnp.dot(ucb, wi_ref[hd], preferred_element_type=F32) + bi_ref[:, cs])
        log_a = -LRU_C * r * jax.nn.softplus(-lam_ref[:, cs])
        a = jnp.exp(log_a)
        a_ref[:, cs] = a
        b_ref[:, cs] = jnp.sqrt(-jnp.tanh(log_a) * (a * a + 1.0)) * (ig * uc)

    ext_ref[0:halo, :] = ext_ref[ts:ts + halo, :]

    width = a_ref.shape[1]
    sub = 8
    row = lax.broadcasted_iota(jnp.int32, (sub, scan_w), 0)
    for c in range(width // scan_w):
        cs = pl.ds(c * scan_w, scan_w)

        def group(gi, h, cs=cs):
            rr = pl.ds(pl.multiple_of(gi * sub, sub), sub)
            a = a_ref[rr, cs]
            b = b_ref[rr, cs]
            for dist in (1, 2, 4):
                keep = row >= dist
                b = b + a * jnp.where(keep, pltpu.roll(b, dist, 0), 0.0)
                a = a * jnp.where(keep, pltpu.roll(a, dist, 0), 1.0)
            hh = a * h + b
            b_ref[rr, cs] = hh
            return hh[sub - 1:sub, :]

        h_ref[0:1, cs] = lax.fori_loop(0, ts // sub, group, h_ref[0:1, cs], unroll=2)

    hg_ref[...] = (b_ref[...] * gg_ref[...].astype(F32)).astype(hg_ref.dtype)


def _lru(u, gg, conv_w, conv_b, w_r, b_r, w_i, b_i, lam, batch, ts):
    t, w = u.shape
    n_blocks, block_w, _ = w_r.shape
    ns = t // batch // ts
    row = lambda b, s: (b * ns + s, 0)
    full2 = lambda b, s: (0, 0)
    full3 = lambda b, s: (0, 0, 0)
    kern = functools.partial(_lru_kernel, ts=ts, n_blocks=n_blocks, block_w=block_w,
                             scan_w=min(w, 1024))
    return pl.pallas_call(
        kern,
        grid=(batch, ns),
        in_specs=[
            pl.BlockSpec((ts, w), row),
            pl.BlockSpec((ts, w), row),
            pl.BlockSpec((CONV_WIDTH, w), full2),
            pl.BlockSpec((1, w), full2),
            pl.BlockSpec((n_blocks, block_w, block_w), full3),
            pl.BlockSpec((1, w), full2),
            pl.BlockSpec((n_blocks, block_w, block_w), full3),
            pl.BlockSpec((1, w), full2),
            pl.BlockSpec((1, w), full2),
        ],
        out_specs=pl.BlockSpec((ts, w), row),
        out_shape=jax.ShapeDtypeStruct((t, w), BF16),
        scratch_shapes=[
            pltpu.VMEM((ts + 8, w), F32),
            pltpu.VMEM((ts, w), F32),
            pltpu.VMEM((ts, w), F32),
            pltpu.VMEM((8, w), F32),
        ],
        compiler_params=_params("arbitrary", "arbitrary"),
        name="l0_rglru",
    )(u, gg, conv_w, conv_b.reshape(1, w), w_r, b_r.reshape(1, w), w_i, b_i.reshape(1, w),
      lam.reshape(1, w))


def _mm_ln_kernel(a_ref, w_ref, r_ref, g_ref, b_ref, *rest, nk, alpha, n_experts):
    if n_experts:
        wr_ref, br_ref, o_ref, route_ref, acc_ref, mu_ref, rstd_ref = rest
    else:
        o_ref, acc_ref, mu_ref, rstd_ref = rest
    k = pl.program_id(1)

    @pl.when(k == 0)
    def _():
        acc_ref[...] = jnp.dot(a_ref[...], w_ref[...], preferred_element_type=F32)

    @pl.when(jnp.logical_and(k > 0, k < nk - 1))
    def _():
        acc_ref[...] += jnp.dot(a_ref[...], w_ref[...], preferred_element_type=F32)

    @pl.when(k == nk - 1)
    def _():
        y = (acc_ref[...] + jnp.dot(a_ref[...], w_ref[...], preferred_element_type=F32)
             + alpha * r_ref[...].astype(F32))
        acc_ref[...] = y
        mu_ref[...] = jnp.mean(y, axis=-1, keepdims=True)
        _ln_finish(acc_ref, mu_ref, rstd_ref, o_ref, g_ref, b_ref)
        if n_experts:
            logits = jnp.dot(o_ref[...].astype(BF16), wr_ref[...],
                             preferred_element_type=F32) + br_ref[...]
            route_ref[...] = _top2_route(logits, n_experts)


def _top2_route(logits, n_experts):
    lane = lax.broadcasted_iota(jnp.int32, logits.shape, 1)
    neg = jnp.float32(-jnp.inf)
    lg = jnp.where(lane < n_experts, logits, neg)
    m1 = jnp.max(lg, axis=-1, keepdims=True)
    i1 = jnp.min(jnp.where(lg == m1, lane, LANES), axis=-1, keepdims=True)
    lg2 = jnp.where(lane == i1, neg, lg)
    m2 = jnp.max(lg2, axis=-1, keepdims=True)
    i2 = jnp.min(jnp.where(lg2 == m2, lane, LANES), axis=-1, keepdims=True)
    e = jnp.exp(m2 - m1)
    w1 = 1.0 / (1.0 + e)
    w2 = e / (1.0 + e)
    out = jnp.where(lane == 0, i1.astype(F32), 0.0)
    out = jnp.where(lane == 1, i2.astype(F32), out)
    out = jnp.where(lane == 2, w1, out)
    out = jnp.where(lane == 3, w2, out)
    return out


def _mm_ln(a, w, resid, g, b, alpha, tm, tk, router=None, out_dtype=BF16):
    t, kdim = a.shape
    n = w.shape[1]
    nk = kdim // tk
    assert nk >= 2
    n_experts = 0
    in_specs = [
        pl.BlockSpec((tm, tk), lambda i, k: (i, k)),
        pl.BlockSpec((tk, n), lambda i, k: (k, 0)),
        pl.BlockSpec((tm, n), lambda i, k: (i, 0)),
        pl.BlockSpec((1, n), lambda i, k: (0, 0)),
        pl.BlockSpec((1, n), lambda i, k: (0, 0)),
    ]
    args = [a, w, resid, g.reshape(1, n), b.reshape(1, n)]
    out_specs = [pl.BlockSpec((tm, n), lambda i, k: (i, 0))]
    out_shape = [jax.ShapeDtypeStruct((t, n), out_dtype)]
    if router is not None:
        w_router, b_router = router
        n_experts = w_router.shape[1]
        wr = jnp.zeros((n, LANES), BF16).at[:, :n_experts].set(w_router.astype(BF16))
        br = jnp.zeros((1, LANES), F32).at[0, :n_experts].set(b_router.astype(F32))
        in_specs += [pl.BlockSpec((n, LANES), lambda i, k: (0, 0)),
                     pl.BlockSpec((1, LANES), lambda i, k: (0, 0))]
        args += [wr, br]
        out_specs.append(pl.BlockSpec((tm, LANES), lambda i, k: (i, 0)))
        out_shape.append(jax.ShapeDtypeStruct((t, LANES), F32))
    kern = functools.partial(_mm_ln_kernel, nk=nk, alpha=alpha, n_experts=n_experts)
    out = pl.pallas_call(
        kern,
        grid=(t // tm, nk),
        in_specs=in_specs,
        out_specs=out_specs,
        out_shape=out_shape,
        scratch_shapes=[pltpu.VMEM((tm, n), F32), pltpu.VMEM((tm, 1), F32), pltpu.VMEM((tm, 1), F32)],
        compiler_params=_params("parallel", "arbitrary"),
        name="mm_res_ln",
    )(*args)
    return out if router is not None else out[0]


def _swa_kernel(sink_ref, q_ref, k_ref, v_ref, kp_ref, vp_ref, o_ref, *, rows, n_kv, group,
                tiles_per_seq):
    i = pl.program_id(0)
    nqb = rows // WINDOW
    gq = group * WINDOW
    scale = HEAD_DIM ** -0.5
    neg = jnp.float32(-jnp.inf)

    qi = lax.broadcasted_iota(jnp.int32, (gq, 2 * WINDOW), 0) % WINDOW
    kj = lax.broadcasted_iota(jnp.int32, (gq, 2 * WINDOW), 1)
    delta = qi + WINDOW - kj
    band = (delta >= 0) & (delta < WINDOW)
    bias = jnp.where(band, 0.0, neg)
    seq_start = (i % tiles_per_seq) == 0
    bias_first = jnp.where(jnp.logical_and(seq_start, kj < WINDOW), neg, bias)
    grp = lax.broadcasted_iota(jnp.int32, (gq, 1), 0) // WINDOW

    for hk in range(n_kv):
        ks = slice(hk * HEAD_DIM, (hk + 1) * HEAD_DIM)
        sink = jnp.zeros((gq, 1), F32)
        for g in range(group):
            sink = jnp.where(grp == g, sink_ref[hk * group + g], sink)
        for qb in range(nqb):
            rs = slice(qb * WINDOW, (qb + 1) * WINDOW)
            if qb == 0:
                k_prev, v_prev = kp_ref[:, ks], vp_ref[:, ks]
            else:
                ps = slice((qb - 1) * WINDOW, qb * WINDOW)
                k_prev, v_prev = k_ref[ps, ks], v_ref[ps, ks]
            k_win = jnp.concatenate([k_prev, k_ref[rs, ks]], axis=0)
            v_win = jnp.concatenate([v_prev, v_ref[rs, ks]], axis=0)
            q4 = jnp.concatenate(
                [q_ref[rs, (hk * group + g) * HEAD_DIM:(hk * group + g + 1) * HEAD_DIM]
                 for g in range(group)], axis=0)
            sc = lax.dot_general(q4, k_win, (((1,), (1,)), ((), ())),
                                 preferred_element_type=F32) * scale
            sc = sc + (bias_first if qb == 0 else bias)
            m = jnp.maximum(jnp.max(sc, axis=-1, keepdims=True), sink)
            p = jnp.exp(sc - m)
            denom = jnp.sum(p, axis=-1, keepdims=True) + jnp.exp(sink - m)
            o = jnp.dot((p / denom).astype(v_win.dtype), v_win, preferred_element_type=F32)
            for g in range(group):
                hs = slice((hk * group + g) * HEAD_DIM, (hk * group + g + 1) * HEAD_DIM)
                o_ref[rs, hs] = o[g * WINDOW:(g + 1) * WINDOW, :].astype(o_ref.dtype)


def _swa(qkv, sinks, n_q, n_kv, seq, rows):
    t = qkv.shape[0]
    dq = n_q * HEAD_DIM
    dkv = n_kv * HEAD_DIM
    assert dq % dkv == 0
    k_blk = dq // dkv
    nqb = rows // WINDOW
    kern = functools.partial(_swa_kernel, rows=rows, n_kv=n_kv, group=n_q // n_kv,
                             tiles_per_seq=seq // rows)
    prev = lambda c: (lambda i, s: (jnp.maximum(i * nqb - 1, 0), c))
    grid_spec = pltpu.PrefetchScalarGridSpec(
        num_scalar_prefetch=1,
        grid=(t // rows,),
        in_specs=[
            pl.BlockSpec((rows, dq), lambda i, s: (i, 0)),
            pl.BlockSpec((rows, dkv), lambda i, s: (i, k_blk)),
            pl.BlockSpec((rows, dkv), lambda i, s: (i, k_blk + 1)),
            pl.BlockSpec((WINDOW, dkv), prev(k_blk)),
            pl.BlockSpec((WINDOW, dkv), prev(k_blk + 1)),
        ],
        out_specs=pl.BlockSpec((rows, dq), lambda i, s: (i, 0)),
    )
    return pl.pallas_call(
        kern,
        grid_spec=grid_spec,
        out_shape=jax.ShapeDtypeStruct((t, dq), BF16),
        compiler_params=_params("arbitrary"),
        name="l1_swa",
    )(sinks.astype(F32), qkv, qkv, qkv, qkv, qkv)


def _cast_kernel(x_ref, o_ref):
    o_ref[...] = x_ref[...].astype(o_ref.dtype)


def _cast_bf16(x, tm):
    r, c = x.shape
    return pl.pallas_call(
        _cast_kernel,
        grid=(r // tm,),
        in_specs=[pl.BlockSpec((tm, c), lambda i: (i, 0))],
        out_specs=pl.BlockSpec((tm, c), lambda i: (i, 0)),
        out_shape=jax.ShapeDtypeStruct((r, c), BF16),
        compiler_params=_params("parallel"),
        name="cast_bf16",
    )(x)


def _row_copy(src_ref, idx, dst_ref, r, sem):
    return pltpu.make_async_copy(src_ref.at[pl.ds(idx, 1), :], dst_ref.at[pl.ds(r, 1), :], sem)


def _gather_rows_kernel(idx_ref, idx_next_ref, src_ref, o_ref, buf_ref, sem):
    i = pl.program_id(0)
    n = pl.num_programs(0)
    rows = o_ref.shape[0]
    slot = i % 2

    def start_tile(p_ref, s):
        def body(r, c):
            _row_copy(src_ref, p_ref[0, 0, r], buf_ref.at[s], r, sem.at[s]).start()
            return c
        lax.fori_loop(0, rows, body, 0, unroll=8)

    @pl.when(i == 0)
    def _():
        start_tile(idx_ref, 0)

    @pl.when(i + 1 < n)
    def _():
        start_tile(idx_next_ref, 1 - slot)

    def wait(r, c):
        _row_copy(src_ref, 0, buf_ref.at[slot], r, sem.at[slot]).wait()
        return c

    lax.fori_loop(0, rows, wait, 0, unroll=8)
    o_ref[...] = buf_ref[slot].astype(o_ref.dtype)


def _gather_rows(src, idx, tm, out_dtype):
    p = idx.shape[0]
    d = src.shape[1]
    nt = p // tm
    idx3 = idx.reshape(nt, 1, tm)
    return pl.pallas_call(
        _gather_rows_kernel,
        grid=(nt,),
        in_specs=[
            pl.BlockSpec((1, 1, tm), lambda i: (i, 0, 0), memory_space=pltpu.SMEM),
            pl.BlockSpec((1, 1, tm), lambda i: (jnp.minimum(i + 1, nt - 1), 0, 0),
                         memory_space=pltpu.SMEM),
            pl.BlockSpec(memory_space=pl.ANY),
        ],
        out_specs=pl.BlockSpec((tm, d), lambda i: (i, 0)),
        out_shape=jax.ShapeDtypeStruct((p, d), out_dtype),
        scratch_shapes=[pltpu.VMEM((2, tm, d), src.dtype), pltpu.SemaphoreType.DMA((2,))],
        compiler_params=_params("arbitrary"),
        name="moe_gather_rows",
    )(idx3, idx3, src)


def _new_expert(te_ref):
    i = pl.program_id(1)
    return jnp.logical_or(i == 0, te_ref[i] != te_ref[jnp.maximum(i - 1, 0)])


def _moe_up_kernel(te_ref, nv_ref, x_ref, wg_ref, wu_ref, h_ref, wgb_ref, wub_ref):
    _load_weights(_new_expert(te_ref), (wg_ref, wu_ref), (wgb_ref, wub_ref))
    valid = pl.program_id(1) < nv_ref[0]

    @pl.when(valid)
    def _():
        x = x_ref[...]
        g = jnp.dot(x, wgb_ref[...], preferred_element_type=F32)
        u = jnp.dot(x, wub_ref[...], preferred_element_type=F32)
        h_ref[...] = (jax.nn.silu(g) * u).astype(h_ref.dtype)

    @pl.when(jnp.logical_not(valid))
    def _():
        h_ref[...] = jnp.zeros_like(h_ref)


def _moe_up(tile_expert, n_valid, xs, we_gate_up, tm, tn):
    p, d = xs.shape
    f = we_gate_up.shape[2] // 2
    nj = f // tn
    row = lambda j, i, te, nv: (jnp.minimum(i, nv[0] - 1), 0)
    grid_spec = pltpu.PrefetchScalarGridSpec(
        num_scalar_prefetch=2,
        grid=(nj, p // tm),
        in_specs=[
            pl.BlockSpec((tm, d), row),
            pl.BlockSpec((1, d, tn), lambda j, i, te, nv: (te[i], 0, j)),
            pl.BlockSpec((1, d, tn), lambda j, i, te, nv: (te[i], 0, j + nj)),
        ],
        out_specs=pl.BlockSpec((tm, tn), lambda j, i, te, nv: (i, j)),
        scratch_shapes=[pltpu.VMEM((d, tn), BF16), pltpu.VMEM((d, tn), BF16)],
    )
    return pl.pallas_call(
        _moe_up_kernel,
        grid_spec=grid_spec,
        out_shape=jax.ShapeDtypeStruct((p, f), BF16),
        compiler_params=_params("arbitrary", "arbitrary"),
        name="moe_up",
    )(tile_expert, n_valid, xs, we_gate_up, we_gate_up)


def _moe_down_kernel(te_ref, nv_ref, h_ref, w_ref, y_ref, wb_ref):
    _load_weights(_new_expert(te_ref), (w_ref,), (wb_ref,))
    valid = pl.program_id(1) < nv_ref[0]

    @pl.when(valid)
    def _():
        y_ref[...] = jnp.dot(h_ref[...], wb_ref[...], preferred_element_type=F32).astype(y_ref.dtype)

    @pl.when(jnp.logical_not(valid))
    def _():
        y_ref[...] = jnp.zeros_like(y_ref)


def _moe_down(tile_expert, n_valid, hs, we_down, tm, tn):
    p, f = hs.shape
    d = we_down.shape[2]
    grid_spec = pltpu.PrefetchScalarGridSpec(
        num_scalar_prefetch=2,
        grid=(d // tn, p // tm),
        in_specs=[
            pl.BlockSpec((tm, f), lambda j, i, te, nv: (jnp.minimum(i, nv[0] - 1), 0)),
            pl.BlockSpec((1, f, tn), lambda j, i, te, nv: (te[i], 0, j)),
        ],
        out_specs=pl.BlockSpec((tm, tn), lambda j, i, te, nv: (i, j)),
        scratch_shapes=[pltpu.VMEM((f, tn), BF16)],
    )
    return pl.pallas_call(
        _moe_down_kernel,
        grid_spec=grid_spec,
        out_shape=jax.ShapeDtypeStruct((p, d), F32),
        compiler_params=_params("arbitrary", "arbitrary"),
        name="moe_down",
    )(tile_expert, n_valid, hs, we_down)


def _moe_combine_kernel(pos_ref, pos_next_ref, x_ref, ys_ref, route_ref, g_ref, b_ref, o_ref,
                        ybuf_ref, acc_ref, mu_ref, rstd_ref, sem, *, alpha):
    i = pl.program_id(0)
    n = pl.num_programs(0)
    rows = acc_ref.shape[0]
    slot = i % 2

    def start_rows(p_ref, s, r0, count):
        for r in range(count):
            for k in range(TOP_K):
                _row_copy(ys_ref, p_ref[0, 0, TOP_K * (r0 + r) + k], ybuf_ref.at[s, k], r0 + r,
                          sem.at[s]).start()

    def wait_tile(s):
        def body(r, c):
            for k in range(TOP_K):
                _row_copy(ys_ref, 0, ybuf_ref.at[s, k], r, sem.at[s]).wait()
            return c
        lax.fori_loop(0, rows, body, 0, unroll=4)

    @pl.when(i == 0)
    def _():
        lax.fori_loop(0, rows // LN_CHUNK,
                      lambda c, carry: (start_rows(pos_ref, 0, c * LN_CHUNK, LN_CHUNK), carry)[1], 0)

    wait_tile(slot)

    for s0 in range(0, rows, LN_SLAB):
        rs = slice(s0, s0 + LN_SLAB)
        y = (alpha * x_ref[rs, :].astype(F32) + route_ref[rs, 2:3] * ybuf_ref[slot, 0, rs, :]
             + route_ref[rs, 3:4] * ybuf_ref[slot, 1, rs, :])
        acc_ref[rs, :] = y
        mu_ref[rs, :] = jnp.mean(y, axis=-1, keepdims=True)

    _ln_finish(acc_ref, mu_ref, rstd_ref, o_ref, g_ref, b_ref,
               per_chunk=lambda r0: start_rows(pos_next_ref, 1 - slot, r0, LN_CHUNK))

    @pl.when(i == n - 1)
    def _():
        wait_tile(1 - slot)


def _moe_combine(x, ys, pos, route, g, b, alpha, tm, out_dtype):
    t, d = x.shape
    nt = t // tm
    row = lambda i: (i, 0)
    pos3 = pos.reshape(nt, 1, TOP_K * tm)
    kern = functools.partial(_moe_combine_kernel, alpha=alpha)
    return pl.pallas_call(
        kern,
        grid=(nt,),
        in_specs=[
            pl.BlockSpec((1, 1, TOP_K * tm), lambda i: (i, 0, 0), memory_space=pltpu.SMEM),
            pl.BlockSpec((1, 1, TOP_K * tm), lambda i: (jnp.minimum(i + 1, nt - 1), 0, 0),
                         memory_space=pltpu.SMEM),
            pl.BlockSpec((tm, d), row),
            pl.BlockSpec(memory_space=pl.ANY),
            pl.BlockSpec((tm, LANES), row),
            pl.BlockSpec((1, d), lambda i: (0, 0)),
            pl.BlockSpec((1, d), lambda i: (0, 0)),
        ],
        out_specs=pl.BlockSpec((tm, d), row),
        out_shape=jax.ShapeDtypeStruct((t, d), out_dtype),
        scratch_shapes=[
            pltpu.VMEM((2, TOP_K, tm, d), ys.dtype),
            pltpu.VMEM((tm, d), F32),
            pltpu.VMEM((tm, 1), F32),
            pltpu.VMEM((tm, 1), F32),
            pltpu.SemaphoreType.DMA((2,)),
        ],
        compiler_params=_params("arbitrary"),
        name="moe_combine_ln",
    )(pos3, pos3, x, ys, route, g.reshape(1, d), b.reshape(1, d))


def _route_plan(route, n_experts, tm):
    t = route.shape[0]
    e = route[:, :TOP_K].astype(jnp.int32).reshape(-1)
    onehot = (e[:, None] == jnp.arange(n_experts, dtype=jnp.int32)[None, :]).astype(jnp.int32)
    cum = jnp.cumsum(onehot, axis=0)
    rank = jnp.take_along_axis(cum, e[:, None], axis=1)[:, 0] - 1
    counts = cum[-1]
    padded = ((counts + tm - 1) // tm) * tm
    ends = jnp.cumsum(padded)
    pos = (ends - padded)[e] + rank
    p_rows = TOP_K * t + n_experts * tm
    row_token = jnp.zeros((p_rows,), jnp.int32).at[pos].set(jnp.arange(TOP_K * t, dtype=jnp.int32) // TOP_K)
    tile_start = jnp.arange(p_rows // tm, dtype=jnp.int32) * tm
    n_valid = (ends[-1] // tm).astype(jnp.int32).reshape(1)
    last_start = jnp.minimum(tile_start, ends[-1] - tm)
    tile_expert = jnp.minimum(jnp.searchsorted(ends, last_start, side="right"), n_experts - 1).astype(jnp.int32)
    return pos.reshape(t, TOP_K), row_token, tile_expert, n_valid


def _tile(dim, want):
    return min(dim, want)


def kernel(x, l0_w_in, l0_conv_w, l0_conv_b, l0_w_rgate, l0_b_rgate, l0_w_igate, l0_b_igate, l0_lru_lambda, l0_w_out, l0_ln1_g, l0_ln1_b, l0_w_gate_up, l0_w_down, l0_ln2_g, l0_ln2_b, l1_w_qkv, l1_b_qkv, l1_sinks, l1_w_o, l1_ln1_g, l1_ln1_b, l1_w_router, l1_b_router, l1_we_gate_up, l1_we_down, l1_ln2_g, l1_ln2_b):
    batch, seq, d = x.shape
    t = batch * seq
    depth = 2
    alpha = (2 * depth) ** 0.25
    n_q = l1_sinks.shape[0]
    n_kv = (l1_w_qkv.shape[1] // HEAD_DIM - n_q) // 2
    n_experts = l1_w_router.shape[1]

    x2d = x.reshape(t, d)
    tm_cast = _tile(d, 512)
    xb = _cast_bf16(x2d, tm_cast)
    bf = lambda w: w.astype(BF16)
    w_out_b = _cast_bf16(l0_w_out, tm_cast)
    w_down_b = _cast_bf16(l0_w_down, tm_cast)
    w_o_b = _cast_bf16(l1_w_o, tm_cast)

    tm_big = _tile(t, 1024)
    tm_ln = _tile(t, 512)

    tn_pair = _tile(d // 2, 512)
    u, gg = _pair_matmul(xb, l0_w_in, tm_big, tn_pair, _proj_epilogue, (F32, BF16), "l0_proj")
    hg = _lru(u, gg, l0_conv_w, l0_conv_b, bf(l0_w_rgate), l0_b_rgate, bf(l0_w_igate), l0_b_igate,
              l0_lru_lambda, batch, _tile(seq, 256))
    x1 = _mm_ln(hg, w_out_b, x2d, l0_ln1_g, l0_ln1_b, alpha, tm_ln, _tile(d // 2, 1024))
    (h,) = _pair_matmul(x1, l0_w_gate_up, tm_big, tn_pair, _swiglu_epilogue, (BF16,), "swiglu_up")
    x2 = _mm_ln(h, w_down_b, x1, l0_ln2_g, l0_ln2_b, alpha, tm_ln, _tile(h.shape[1] // 2, 1024))

    pos = jnp.arange(seq, dtype=F32)
    inv = ROPE_THETA ** (-jnp.arange(0, HEAD_DIM, 2, dtype=F32) / HEAD_DIM)
    ang = pos[:, None] * inv[None, :]
    cos_f = jnp.concatenate([jnp.cos(ang), jnp.cos(ang)], axis=-1)
    sin_f = jnp.concatenate([-jnp.sin(ang), jnp.sin(ang)], axis=-1)
    qkv = _qkv(x2, l1_w_qkv, l1_b_qkv, cos_f, sin_f, (n_q + n_kv) * HEAD_DIM,
               _tile(seq, 1024), _tile(n_kv * HEAD_DIM, 512))
    o = _swa(qkv, l1_sinks, n_q, n_kv, seq, _tile(seq, 512))
    x3, route = _mm_ln(o, w_o_b, x2, l1_ln1_g, l1_ln1_b, alpha, tm_ln, _tile(d // 2, 1024),
                       router=(l1_w_router, l1_b_router), out_dtype=F32)

    tm_e = _tile(t, 512)
    pos2, row_token, tile_expert, n_valid = _route_plan(route, n_experts, tm_e)
    xs = _gather_rows(x3, row_token, tm_e, BF16)
    hs = _moe_up(tile_expert, n_valid, xs, l1_we_gate_up, tm_e, _tile(d // 2, 512))
    ys = _moe_down(tile_expert, n_valid, hs, l1_we_down, tm_e, _tile(d // 2, 1024))
    out = _moe_combine(x3, ys, pos2, route, l1_ln2_g, l1_ln2_b, alpha, _tile(t, 256), x.dtype)
    return out.reshape(batch, seq, d)
```

```python
import functools

import jax
import jax.numpy as jnp
from jax import lax
from jax.experimental import pallas as pl
from jax.experimental.pallas import tpu as pltpu

F32 = jnp.float32
BF16 = jnp.bfloat16

HEAD_DIM = 128
WINDOW = 128
CONV_WIDTH = 4
LRU_C = 8.0
TOP_K = 2
ROPE_THETA = 10000.0
LN_EPS = 1e-5
LANES = 128
VMEM_LIMIT_BYTES = 56 * 1024 * 1024
LN_SLAB = 64
LN_CHUNK = 16


def _params(*sem):
    return pltpu.CompilerParams(dimension_semantics=sem, vmem_limit_bytes=VMEM_LIMIT_BYTES)


def _pack_pair(lo, hi):
    lo_bits = lax.bitcast_convert_type(lo.astype(BF16).astype(F32), jnp.uint32)
    hi_bits = lax.bitcast_convert_type(hi.astype(BF16).astype(F32), jnp.uint32)
    return (lo_bits >> 16) | hi_bits


def _unpack_pair(w):
    lo = lax.bitcast_convert_type(w << 16, F32)
    hi = lax.bitcast_convert_type(w & jnp.uint32(0xFFFF0000), F32)
    return lo, hi


def _pack_rows(o, group):
    half = group // 2
    return jnp.concatenate(
        [_pack_pair(o[:, g:g + half], o[:, g + half:g + group]) for g in range(0, o.shape[1], group)],
        axis=1)


def _ln_finish(y_ref, mu_ref, rstd_ref, o_ref, g_ref, b_ref, per_chunk=None, store=None):
    rows = y_ref.shape[0]
    inv_n = 1.0 / y_ref.shape[1]
    for s0 in range(0, rows, LN_SLAB):
        rs = slice(s0, s0 + LN_SLAB)
        d = y_ref[rs, :] - mu_ref[rs, :]
        var = jnp.sum(d * d, axis=-1, keepdims=True) * inv_n
        rstd_ref[rs, :] = lax.rsqrt(var + LN_EPS)

    def body(c, carry):
        r0 = pl.multiple_of(c * LN_CHUNK, LN_CHUNK)
        rr = pl.ds(r0, LN_CHUNK)
        o = (y_ref[rr, :] - mu_ref[rr, :]) * rstd_ref[rr, :] * g_ref[...] + b_ref[...]
        if store is None:
            o_ref[rr, :] = o.astype(o_ref.dtype)
        else:
            store(rr, o)
        if per_chunk is not None:
            per_chunk(r0)
        return carry

    lax.fori_loop(0, rows // LN_CHUNK, body, 0, unroll=2)


def _load_weights(first, w_refs, wb_refs):
    @pl.when(first)
    def _():
        for w_ref, wb_ref in zip(w_refs, wb_refs):
            w = w_ref[0] if len(w_ref.shape) == 3 else w_ref[...]
            wb_ref[...] = w.astype(wb_ref.dtype)


def _pair_kernel(x_ref, w_hbm, *rest, epilogue, n_out, nj, ni):
    outs = rest[:n_out]
    wb_ref, stage_ref, sem = rest[n_out:]
    j = pl.program_id(0)
    i = pl.program_id(1)
    ks = stage_ref.shape[2]
    tn = stage_ref.shape[3]
    cur = j % 2
    nxt = 1 - cur

    def slab_copy(jj, s, m, st):
        col = pl.multiple_of((jj + m * nj) * tn, tn)
        row = pl.multiple_of(s * ks, ks)
        return pltpu.make_async_copy(w_hbm.at[pl.ds(row, ks), pl.ds(col, tn)], stage_ref.at[st, m],
                                     sem.at[st, m])

    def land(jj, s, slot_w):
        st = s % 2
        for m in range(2):
            slab_copy(jj, s, m, st).wait()
            wb_ref[slot_w, m, pl.ds(pl.multiple_of(s * ks, ks), ks), :] = (
                stage_ref[st, m].astype(wb_ref.dtype))

    @pl.when(jnp.logical_and(j == 0, i == 0))
    def _():
        def body(s, c):
            for m in range(2):
                slab_copy(0, s, m, s % 2).start()
            land(0, s, 0)
            return c
        lax.fori_loop(0, ni, body, 0)

    has_next = j + 1 < nj

    @pl.when(has_next)
    def _():
        for m in range(2):
            slab_copy(j + 1, i, m, i % 2).start()

    @pl.when(jnp.logical_and(has_next, i > 0))
    def _():
        land(j + 1, i - 1, nxt)

    x = x_ref[...]
    y0 = jnp.dot(x, wb_ref[cur, 0], preferred_element_type=F32)
    y1 = jnp.dot(x, wb_ref[cur, 1], preferred_element_type=F32)
    epilogue(y0, y1, *outs)

    @pl.when(jnp.logical_and(has_next, i == ni - 1))
    def _():
        land(j + 1, ni - 1, nxt)


def _pair_matmul(xb, w, tm, tn, epilogue, out_dtypes, name):
    t, d = xb.shape
    f = w.shape[1] // 2
    nj = f // tn
    ni = t // tm
    ks = d // ni
    assert ks * ni == d and ks % 16 == 0
    kern = functools.partial(_pair_kernel, epilogue=epilogue, n_out=len(out_dtypes), nj=nj, ni=ni)
    return pl.pallas_call(
        kern,
        grid=(nj, ni),
        in_specs=[
            pl.BlockSpec((tm, d), lambda j, i: (i, 0)),
            pl.BlockSpec(memory_space=pl.ANY),
        ],
        out_specs=[pl.BlockSpec((tm, tn), lambda j, i: (i, j)) for _ in out_dtypes],
        out_shape=[jax.ShapeDtypeStruct((t, f), dt) for dt in out_dtypes],
        scratch_shapes=[
            pltpu.VMEM((2, 2, d, tn), BF16),
            pltpu.VMEM((2, 2, ks, tn), w.dtype),
            pltpu.SemaphoreType.DMA((2, 2)),
        ],
        compiler_params=_params("arbitrary", "arbitrary"),
        name=name,
    )(xb, w)


def _proj_epilogue(u, g, u_ref, gg_ref):
    u_ref[...] = u
    gg_ref[...] = jax.nn.gelu(g, approximate=True).astype(gg_ref.dtype)


def _swiglu_epilogue(g, u, h_ref):
    h_ref[...] = (jax.nn.silu(g) * u).astype(h_ref.dtype)


def _qkv_kernel(x_ref, w_ref, b_ref, cos_ref, sin_ref, o_ref, wb_ref, *, n_rope_tiles):
    _load_weights(pl.program_id(1) == 0, (w_ref,), (wb_ref,))
    acc = jnp.dot(x_ref[...], wb_ref[...], preferred_element_type=F32) + b_ref[...]
    rope = pl.program_id(0) < n_rope_tiles
    cos = jnp.where(rope, cos_ref[...], 1.0)
    sin = jnp.where(rope, sin_ref[...], 0.0)
    for c in range(acc.shape[1] // HEAD_DIM):
        cs = slice(c * HEAD_DIM, (c + 1) * HEAD_DIM)
        tt = acc[:, cs]
        o_ref[:, cs] = (tt * cos + pltpu.roll(tt, HEAD_DIM // 2, 1) * sin).astype(o_ref.dtype)


def _qkv(xb, w_qkv, b_qkv, cos_f, sin_f, n_rope_cols, tm, tn):
    t, d = xb.shape
    n = w_qkv.shape[1]
    seq = cos_f.shape[0]
    n_pos_tiles = seq // tm
    kern = functools.partial(_qkv_kernel, n_rope_tiles=n_rope_cols // tn)
    return pl.pallas_call(
        kern,
        grid=(n // tn, t // tm),
        in_specs=[
            pl.BlockSpec((tm, d), lambda j, i: (i, 0)),
            pl.BlockSpec((d, tn), lambda j, i: (0, j)),
            pl.BlockSpec((1, tn), lambda j, i: (0, j)),
            pl.BlockSpec((tm, HEAD_DIM), lambda j, i: (i % n_pos_tiles, 0)),
            pl.BlockSpec((tm, HEAD_DIM), lambda j, i: (i % n_pos_tiles, 0)),
        ],
        out_specs=pl.BlockSpec((tm, tn), lambda j, i: (i, j)),
        out_shape=jax.ShapeDtypeStruct((t, n), BF16),
        scratch_shapes=[pltpu.VMEM((d, tn), BF16)],
        compiler_params=_params("arbitrary", "arbitrary"),
        name="l1_qkv_rope",
    )(xb, w_qkv, b_qkv.reshape(1, n), cos_f, sin_f)


def _lru_kernel(u_ref, gg_ref, cw_ref, cb_ref, wr_ref, br_ref, wi_ref, bi_ref, lam_ref,
                hg_ref, ext_ref, a_ref, b_ref, h_ref, *, ts, n_blocks, block_w, scan_w):
    halo = 8
    s = pl.program_id(1)

    @pl.when(s == 0)
    def _():
        ext_ref[0:halo, :] = jnp.zeros((halo, ext_ref.shape[1]), F32)
        h_ref[...] = jnp.zeros_like(h_ref)

    ext_ref[halo:halo + ts, :] = u_ref[...]

    for hd in range(n_blocks):
        cs = slice(hd * block_w, (hd + 1) * block_w)
        uc = cb_ref[:, cs]
        for k in range(CONV_WIDTH):
            off = halo - (CONV_WIDTH - 1) + k
            uc = uc + cw_ref[k:k + 1, cs] * ext_ref[off:off + ts, cs]
        ucb = uc.astype(BF16)
        r = jax.nn.sigmoid(jnp.dot(ucb, wr_ref[hd], preferred_element_type=F32) + br_ref[:, cs])
        ig = jax.nn.sigmoid(jnp.dot(ucb, wi_ref[hd], preferred_element_type=F32) + bi_ref[:, cs])
        log_a = -LRU_C * r * jax.nn.softplus(-lam_ref[:, cs])
        a = jnp.exp(log_a)
        a_ref[:, cs] = a
        b_ref[:, cs] = jnp.sqrt(-jnp.tanh(log_a) * (a * a + 1.0)) * (ig * uc)

    ext_ref[0:halo, :] = ext_ref[ts:ts + halo, :]

    width = a_ref.shape[1]
    sub = 8
    row = lax.broadcasted_iota(jnp.int32, (sub, scan_w), 0)
    for c in range(width // scan_w):
        cs = pl.ds(c * scan_w, scan_w)

        def group(gi, h, cs=cs):
            rr = pl.ds(pl.multiple_of(gi * sub, sub), sub)
            a = a_ref[rr, cs]
            b = b_ref[rr, cs]
            for dist in (1, 2, 4):
                keep = row >= dist
                b = b + a * jnp.where(keep, pltpu.roll(b, dist, 0), 0.0)
                a = a * jnp.where(keep, pltpu.roll(a, dist, 0), 1.0)
            hh = a * h + b
            b_ref[rr, cs] = hh
            return hh[sub - 1:sub, :]

        h_ref[0:1, cs] = lax.fori_loop(0, ts // sub, group, h_ref[0:1, cs], unroll=2)

    hg_ref[...] = (b_ref[...] * gg_ref[...].astype(F32)).astype(hg_ref.dtype)


def _lru(u, gg, conv_w, conv_b, w_r, b_r, w_i, b_i, lam, batch, ts):
    t, w = u.shape
    n_blocks, block_w, _ = w_r.shape
    ns = t // batch // ts
    row = lambda b, s: (b * ns + s, 0)
    full2 = lambda b, s: (0, 0)
    full3 = lambda b, s: (0, 0, 0)
    kern = functools.partial(_lru_kernel, ts=ts, n_blocks=n_blocks, block_w=block_w,
                             scan_w=min(w, 1024))
    return pl.pallas_call(
        kern,
        grid=(batch, ns),
        in_specs=[
            pl.BlockSpec((ts, w), row),
            pl.BlockSpec((ts, w), row),
            pl.BlockSpec((CONV_WIDTH, w), full2),
            pl.BlockSpec((1, w), full2),
            pl.BlockSpec((n_blocks, block_w, block_w), full3),
            pl.BlockSpec((1, w), full2),
            pl.BlockSpec((n_blocks, block_w, block_w), full3),
            pl.BlockSpec((1, w), full2),
            pl.BlockSpec((1, w), full2),
        ],
        out_specs=pl.BlockSpec((ts, w), row),
        out_shape=jax.ShapeDtypeStruct((t, w), BF16),
        scratch_shapes=[
            pltpu.VMEM((ts + 8, w), F32),
            pltpu.VMEM((ts, w), F32),
            pltpu.VMEM((ts, w), F32),
            pltpu.VMEM((8, w), F32),
        ],
        compiler_params=_params("arbitrary", "arbitrary"),
        name="l0_rglru",
    )(u, gg, conv_w, conv_b.reshape(1, w), w_r, b_r.reshape(1, w), w_i, b_i.reshape(1, w),
      lam.reshape(1, w))


def _mm_ln_kernel(a_ref, w_ref, r_ref, g_ref, b_ref, *rest, nk, alpha, n_experts, pack_group):
    if n_experts:
        wr_ref, br_ref, o_ref, route_ref, acc_ref, mu_ref, rstd_ref, xb_ref = rest
    else:
        o_ref, acc_ref, mu_ref, rstd_ref = rest
    k = pl.program_id(1)

    @pl.when(k == 0)
    def _():
        acc_ref[...] = jnp.dot(a_ref[...], w_ref[...], preferred_element_type=F32)

    @pl.when(jnp.logical_and(k > 0, k < nk - 1))
    def _():
        acc_ref[...] += jnp.dot(a_ref[...], w_ref[...], preferred_element_type=F32)

    @pl.when(k == nk - 1)
    def _():
        y = (acc_ref[...] + jnp.dot(a_ref[...], w_ref[...], preferred_element_type=F32)
             + alpha * r_ref[...].astype(F32))
        acc_ref[...] = y
        mu_ref[...] = jnp.mean(y, axis=-1, keepdims=True)
        if n_experts:
            def store(rr, o):
                xb_ref[rr, :] = o.astype(xb_ref.dtype)
                o_ref[rr, :] = _pack_rows(o, pack_group)

            _ln_finish(acc_ref, mu_ref, rstd_ref, o_ref, g_ref, b_ref, store=store)
            logits = jnp.dot(xb_ref[...], wr_ref[...], preferred_element_type=F32) + br_ref[...]
            route_ref[...] = _top2_route(logits, n_experts)
        else:
            _ln_finish(acc_ref, mu_ref, rstd_ref, o_ref, g_ref, b_ref)


def _top2_route(logits, n_experts):
    lane = lax.broadcasted_iota(jnp.int32, logits.shape, 1)
    neg = jnp.float32(-jnp.inf)
    lg = jnp.where(lane < n_experts, logits, neg)
    m1 = jnp.max(lg, axis=-1, keepdims=True)
    i1 = jnp.min(jnp.where(lg == m1, lane, LANES), axis=-1, keepdims=True)
    lg2 = jnp.where(lane == i1, neg, lg)
    m2 = jnp.max(lg2, axis=-1, keepdims=True)
    i2 = jnp.min(jnp.where(lg2 == m2, lane, LANES), axis=-1, keepdims=True)
    e = jnp.exp(m2 - m1)
    w1 = 1.0 / (1.0 + e)
    w2 = e / (1.0 + e)
    out = jnp.where(lane == 0, i1.astype(F32), 0.0)
    out = jnp.where(lane == 1, i2.astype(F32), out)
    out = jnp.where(lane == 2, w1, out)
    out = jnp.where(lane == 3, w2, out)
    return out


def _mm_ln(a, w, resid, g, b, alpha, tm, tk, router=None, pack_group=None):
    t, kdim = a.shape
    n = w.shape[1]
    nk = kdim // tk
    assert nk >= 2
    n_experts = 0
    in_specs = [
        pl.BlockSpec((tm, tk), lambda i, k: (i, k)),
        pl.BlockSpec((tk, n), lambda i, k: (k, 0)),
        pl.BlockSpec((tm, n), lambda i, k: (i, 0)),
        pl.BlockSpec((1, n), lambda i, k: (0, 0)),
        pl.BlockSpec((1, n), lambda i, k: (0, 0)),
    ]
    args = [a, w, resid, g.reshape(1, n), b.reshape(1, n)]
    out_specs = [pl.BlockSpec((tm, n), lambda i, k: (i, 0))]
    out_shape = [jax.ShapeDtypeStruct((t, n), BF16)]
    scratch = [pltpu.VMEM((tm, n), F32), pltpu.VMEM((tm, 1), F32), pltpu.VMEM((tm, 1), F32)]
    if router is not None:
        out_specs = [pl.BlockSpec((tm, n // 2), lambda i, k: (i, 0))]
        out_shape = [jax.ShapeDtypeStruct((t, n // 2), jnp.uint32)]
        scratch.append(pltpu.VMEM((tm, n), BF16))
        w_router, b_router = router
        n_experts = w_router.shape[1]
        wr = jnp.zeros((n, LANES), BF16).at[:, :n_experts].set(w_router.astype(BF16))
        br = jnp.zeros((1, LANES), F32).at[0, :n_experts].set(b_router.astype(F32))
        in_specs += [pl.BlockSpec((n, LANES), lambda i, k: (0, 0)),
                     pl.BlockSpec((1, LANES), lambda i, k: (0, 0))]
        args += [wr, br]
        out_specs.append(pl.BlockSpec((tm, LANES), lambda i, k: (i, 0)))
        out_shape.append(jax.ShapeDtypeStruct((t, LANES), F32))
    kern = functools.partial(_mm_ln_kernel, nk=nk, alpha=alpha, n_experts=n_experts,
                             pack_group=pack_group)
    out = pl.pallas_call(
        kern,
        grid=(t // tm, nk),
        in_specs=in_specs,
        out_specs=out_specs,
        out_shape=out_shape,
        scratch_shapes=scratch,
        compiler_params=_params("parallel", "arbitrary"),
        name="mm_res_ln",
    )(*args)
    return out if router is not None else out[0]


def _swa_kernel(sink_ref, q_ref, k_ref, v_ref, kp_ref, vp_ref, o_ref, *, rows, n_kv, group,
                tiles_per_seq):
    i = pl.program_id(0)
    nqb = rows // WINDOW
    gq = group * WINDOW
    scale = HEAD_DIM ** -0.5
    neg = jnp.float32(-jnp.inf)

    qi = lax.broadcasted_iota(jnp.int32, (gq, 2 * WINDOW), 0) % WINDOW
    kj = lax.broadcasted_iota(jnp.int32, (gq, 2 * WINDOW), 1)
    delta = qi + WINDOW - kj
    band = (delta >= 0) & (delta < WINDOW)
    bias = jnp.where(band, 0.0, neg)
    seq_start = (i % tiles_per_seq) == 0
    bias_first = jnp.where(jnp.logical_and(seq_start, kj < WINDOW), neg, bias)
    grp = lax.broadcasted_iota(jnp.int32, (gq, 1), 0) // WINDOW

    for hk in range(n_kv):
        ks = slice(hk * HEAD_DIM, (hk + 1) * HEAD_DIM)
        sink = jnp.zeros((gq, 1), F32)
        for g in range(group):
            sink = jnp.where(grp == g, sink_ref[hk * group + g], sink)
        for qb in range(nqb):
            rs = slice(qb * WINDOW, (qb + 1) * WINDOW)
            if qb == 0:
                k_prev, v_prev = kp_ref[:, ks], vp_ref[:, ks]
            else:
                ps = slice((qb - 1) * WINDOW, qb * WINDOW)
                k_prev, v_prev = k_ref[ps, ks], v_ref[ps, ks]
            k_win = jnp.concatenate([k_prev, k_ref[rs, ks]], axis=0)
            v_win = jnp.concatenate([v_prev, v_ref[rs, ks]], axis=0)
            q4 = jnp.concatenate(
                [q_ref[rs, (hk * group + g) * HEAD_DIM:(hk * group + g + 1) * HEAD_DIM]
                 for g in range(group)], axis=0)
            sc = lax.dot_general(q4, k_win, (((1,), (1,)), ((), ())),
                                 preferred_element_type=F32) * scale
            sc = sc + (bias_first if qb == 0 else bias)
            m = jnp.maximum(jnp.max(sc, axis=-1, keepdims=True), sink)
            p = jnp.exp(sc - m)
            denom = jnp.sum(p, axis=-1, keepdims=True) + jnp.exp(sink - m)
            o = jnp.dot((p / denom).astype(v_win.dtype), v_win, preferred_element_type=F32)
            for g in range(group):
                hs = slice((hk * group + g) * HEAD_DIM, (hk * group + g + 1) * HEAD_DIM)
                o_ref[rs, hs] = o[g * WINDOW:(g + 1) * WINDOW, :].astype(o_ref.dtype)


def _swa(qkv, sinks, n_q, n_kv, seq, rows):
    t = qkv.shape[0]
    dq = n_q * HEAD_DIM
    dkv = n_kv * HEAD_DIM
    assert dq % dkv == 0
    k_blk = dq // dkv
    nqb = rows // WINDOW
    kern = functools.partial(_swa_kernel, rows=rows, n_kv=n_kv, group=n_q // n_kv,
                             tiles_per_seq=seq // rows)
    prev = lambda c: (lambda i, s: (jnp.maximum(i * nqb - 1, 0), c))
    grid_spec = pltpu.PrefetchScalarGridSpec(
        num_scalar_prefetch=1,
        grid=(t // rows,),
        in_specs=[
            pl.BlockSpec((rows, dq), lambda i, s: (i, 0)),
            pl.BlockSpec((rows, dkv), lambda i, s: (i, k_blk)),
            pl.BlockSpec((rows, dkv), lambda i, s: (i, k_blk + 1)),
            pl.BlockSpec((WINDOW, dkv), prev(k_blk)),
            pl.BlockSpec((WINDOW, dkv), prev(k_blk + 1)),
        ],
        out_specs=pl.BlockSpec((rows, dq), lambda i, s: (i, 0)),
    )
    return pl.pallas_call(
        kern,
        grid_spec=grid_spec,
        out_shape=jax.ShapeDtypeStruct((t, dq), BF16),
        compiler_params=_params("arbitrary"),
        name="l1_swa",
    )(sinks.astype(F32), qkv, qkv, qkv, qkv, qkv)


def _cast_kernel(x_ref, o_ref):
    o_ref[...] = x_ref[...].astype(o_ref.dtype)


def _cast_bf16(x, tm):
    r, c = x.shape
    return pl.pallas_call(
        _cast_kernel,
        grid=(r // tm,),
        in_specs=[pl.BlockSpec((tm, c), lambda i: (i, 0))],
        out_specs=pl.BlockSpec((tm, c), lambda i: (i, 0)),
        out_shape=jax.ShapeDtypeStruct((r, c), BF16),
        compiler_params=_params("parallel"),
        name="cast_bf16",
    )(x)


def _row_copy(src_ref, idx, dst_ref, r, sem):
    return pltpu.make_async_copy(src_ref.at[pl.ds(idx, 1), :], dst_ref.at[pl.ds(r, 1), :], sem)


def _gather_rows_kernel(idx_ref, idx_next_ref, src_ref, o_ref, buf_ref, sem, *, group):
    i = pl.program_id(0)
    n = pl.num_programs(0)
    rows = o_ref.shape[0]
    slot = i % 2

    def start_tile(p_ref, s):
        def body(r, c):
            _row_copy(src_ref, p_ref[0, 0, r], buf_ref.at[s], r, sem.at[s]).start()
            return c
        lax.fori_loop(0, rows, body, 0, unroll=8)

    @pl.when(i == 0)
    def _():
        start_tile(idx_ref, 0)

    @pl.when(i + 1 < n)
    def _():
        start_tile(idx_next_ref, 1 - slot)

    def wait(r, c):
        _row_copy(src_ref, 0, buf_ref.at[slot], r, sem.at[slot]).wait()
        return c

    lax.fori_loop(0, rows, wait, 0, unroll=8)
    half = group // 2
    for g in range(0, o_ref.shape[1], group):
        lo, hi = _unpack_pair(buf_ref[slot, :, g // 2:g // 2 + half])
        o_ref[:, g:g + half] = lo.astype(o_ref.dtype)
        o_ref[:, g + half:g + group] = hi.astype(o_ref.dtype)


def _gather_rows(src, idx, tm, group):
    p = idx.shape[0]
    dw = src.shape[1]
    d = 2 * dw
    nt = p // tm
    idx3 = idx.reshape(nt, 1, tm)
    return pl.pallas_call(
        functools.partial(_gather_rows_kernel, group=group),
        grid=(nt,),
        in_specs=[
            pl.BlockSpec((1, 1, tm), lambda i: (i, 0, 0), memory_space=pltpu.SMEM),
            pl.BlockSpec((1, 1, tm), lambda i: (jnp.minimum(i + 1, nt - 1), 0, 0),
                         memory_space=pltpu.SMEM),
            pl.BlockSpec(memory_space=pl.ANY),
        ],
        out_specs=pl.BlockSpec((tm, d), lambda i: (i, 0)),
        out_shape=jax.ShapeDtypeStruct((p, d), BF16),
        scratch_shapes=[pltpu.VMEM((2, tm, dw), src.dtype), pltpu.SemaphoreType.DMA((2,))],
        compiler_params=_params("arbitrary"),
        name="moe_gather_rows",
    )(idx3, idx3, src)


def _new_expert(te_ref):
    i = pl.program_id(1)
    return jnp.logical_or(i == 0, te_ref[i] != te_ref[jnp.maximum(i - 1, 0)])


def _moe_up_kernel(te_ref, nv_ref, x_ref, wg_ref, wu_ref, h_ref, wgb_ref, wub_ref):
    _load_weights(_new_expert(te_ref), (wg_ref, wu_ref), (wgb_ref, wub_ref))
    valid = pl.program_id(1) < nv_ref[0]

    @pl.when(valid)
    def _():
        x = x_ref[...]
        g = jnp.dot(x, wgb_ref[...], preferred_element_type=F32)
        u = jnp.dot(x, wub_ref[...], preferred_element_type=F32)
        h_ref[...] = (jax.nn.silu(g) * u).astype(h_ref.dtype)

    @pl.when(jnp.logical_not(valid))
    def _():
        h_ref[...] = jnp.zeros_like(h_ref)


def _moe_up(tile_expert, n_valid, xs, we_gate_up, tm, tn):
    p, d = xs.shape
    f = we_gate_up.shape[2] // 2
    nj = f // tn
    row = lambda j, i, te, nv: (jnp.minimum(i, nv[0] - 1), 0)
    grid_spec = pltpu.PrefetchScalarGridSpec(
        num_scalar_prefetch=2,
        grid=(nj, p // tm),
        in_specs=[
            pl.BlockSpec((tm, d), row),
            pl.BlockSpec((1, d, tn), lambda j, i, te, nv: (te[i], 0, j)),
            pl.BlockSpec((1, d, tn), lambda j, i, te, nv: (te[i], 0, j + nj)),
        ],
        out_specs=pl.BlockSpec((tm, tn), lambda j, i, te, nv: (i, j)),
        scratch_shapes=[pltpu.VMEM((d, tn), BF16), pltpu.VMEM((d, tn), BF16)],
    )
    return pl.pallas_call(
        _moe_up_kernel,
        grid_spec=grid_spec,
        out_shape=jax.ShapeDtypeStruct((p, f), BF16),
        compiler_params=_params("arbitrary", "arbitrary"),
        name="moe_up",
    )(tile_expert, n_valid, xs, we_gate_up, we_gate_up)


def _moe_down_kernel(te_ref, nv_ref, h_ref, w_ref, y_ref, wb_ref):
    _load_weights(_new_expert(te_ref), (w_ref,), (wb_ref,))
    valid = pl.program_id(1) < nv_ref[0]

    @pl.when(valid)
    def _():
        y = jnp.dot(h_ref[...], wb_ref[...], preferred_element_type=F32)
        y_ref[...] = _pack_rows(y, y.shape[1])

    @pl.when(jnp.logical_not(valid))
    def _():
        y_ref[...] = jnp.zeros_like(y_ref)


def _moe_down(tile_expert, n_valid, hs, we_down, tm, tn):
    p, f = hs.shape
    d = we_down.shape[2]
    grid_spec = pltpu.PrefetchScalarGridSpec(
        num_scalar_prefetch=2,
        grid=(d // tn, p // tm),
        in_specs=[
            pl.BlockSpec((tm, f), lambda j, i, te, nv: (jnp.minimum(i, nv[0] - 1), 0)),
            pl.BlockSpec((1, f, tn), lambda j, i, te, nv: (te[i], 0, j)),
        ],
        out_specs=pl.BlockSpec((tm, tn // 2), lambda j, i, te, nv: (i, j)),
        scratch_shapes=[pltpu.VMEM((f, tn), BF16)],
    )
    return pl.pallas_call(
        _moe_down_kernel,
        grid_spec=grid_spec,
        out_shape=jax.ShapeDtypeStruct((p, d // 2), jnp.uint32),
        compiler_params=_params("arbitrary", "arbitrary"),
        name="moe_down",
    )(tile_expert, n_valid, hs, we_down)


def _moe_combine_kernel(pos_ref, pos_next_ref, x_ref, ys_ref, route_ref, g_ref, b_ref, o_ref,
                        ybuf_ref, acc_ref, mu_ref, rstd_ref, sem, *, alpha, group):
    i = pl.program_id(0)
    n = pl.num_programs(0)
    rows = acc_ref.shape[0]
    slot = i % 2

    def start_rows(p_ref, s, r0, count):
        for r in range(count):
            for k in range(TOP_K):
                _row_copy(ys_ref, p_ref[0, 0, TOP_K * (r0 + r) + k], ybuf_ref.at[s, k], r0 + r,
                          sem.at[s]).start()

    def wait_tile(s):
        def body(r, c):
            for k in range(TOP_K):
                _row_copy(ys_ref, 0, ybuf_ref.at[s, k], r, sem.at[s]).wait()
            return c
        lax.fori_loop(0, rows, body, 0, unroll=4)

    @pl.when(i == 0)
    def _():
        lax.fori_loop(0, rows // LN_CHUNK,
                      lambda c, carry: (start_rows(pos_ref, 0, c * LN_CHUNK, LN_CHUNK), carry)[1], 0)

    wait_tile(slot)

    half = group // 2
    d = acc_ref.shape[1]
    for s0 in range(0, rows, LN_SLAB):
        rs = slice(s0, s0 + LN_SLAB)
        w1 = route_ref[rs, 2:3]
        w2 = route_ref[rs, 3:4]
        total = jnp.zeros((LN_SLAB, 1), F32)
        for g in range(0, d, group):
            ws = slice(g // 2, g // 2 + half)
            parts = zip(_unpack_pair(x_ref[rs, ws]), _unpack_pair(ybuf_ref[slot, 0, rs, ws]),
                        _unpack_pair(ybuf_ref[slot, 1, rs, ws]))
            for c0, (xv, y1, y2) in zip((g, g + half), parts):
                y = alpha * xv + w1 * y1 + w2 * y2
                acc_ref[rs, c0:c0 + half] = y
                total = total + jnp.sum(y, axis=-1, keepdims=True)
        mu_ref[rs, :] = total * (1.0 / d)

    _ln_finish(acc_ref, mu_ref, rstd_ref, o_ref, g_ref, b_ref,
               per_chunk=lambda r0: start_rows(pos_next_ref, 1 - slot, r0, LN_CHUNK))

    @pl.when(i == n - 1)
    def _():
        wait_tile(1 - slot)


def _moe_combine(x, ys, pos, route, g, b, alpha, tm, group, out_dtype):
    t, dw = x.shape
    d = 2 * dw
    nt = t // tm
    row = lambda i: (i, 0)
    pos3 = pos.reshape(nt, 1, TOP_K * tm)
    kern = functools.partial(_moe_combine_kernel, alpha=alpha, group=group)
    return pl.pallas_call(
        kern,
        grid=(nt,),
        in_specs=[
            pl.BlockSpec((1, 1, TOP_K * tm), lambda i: (i, 0, 0), memory_space=pltpu.SMEM),
            pl.BlockSpec((1, 1, TOP_K * tm), lambda i: (jnp.minimum(i + 1, nt - 1), 0, 0),
                         memory_space=pltpu.SMEM),
            pl.BlockSpec((tm, dw), row),
            pl.BlockSpec(memory_space=pl.ANY),
            pl.BlockSpec((tm, LANES), row),
            pl.BlockSpec((1, d), lambda i: (0, 0)),
            pl.BlockSpec((1, d), lambda i: (0, 0)),
        ],
        out_specs=pl.BlockSpec((tm, d), row),
        out_shape=jax.ShapeDtypeStruct((t, d), out_dtype),
        scratch_shapes=[
            pltpu.VMEM((2, TOP_K, tm, dw), ys.dtype),
            pltpu.VMEM((tm, d), F32),
            pltpu.VMEM((tm, 1), F32),
            pltpu.VMEM((tm, 1), F32),
            pltpu.SemaphoreType.DMA((2,)),
        ],
        compiler_params=_params("arbitrary"),
        name="moe_combine_ln",
    )(pos3, pos3, x, ys, route, g.reshape(1, d), b.reshape(1, d))


def _route_plan(route, n_experts, tm):
    t = route.shape[0]
    e = route[:, :TOP_K].astype(jnp.int32).reshape(-1)
    onehot = (e[:, None] == jnp.arange(n_experts, dtype=jnp.int32)[None, :]).astype(jnp.int32)
    cum = jnp.cumsum(onehot, axis=0)
    rank = jnp.take_along_axis(cum, e[:, None], axis=1)[:, 0] - 1
    counts = cum[-1]
    padded = ((counts + tm - 1) // tm) * tm
    ends = jnp.cumsum(padded)
    pos = (ends - padded)[e] + rank
    p_rows = TOP_K * t + n_experts * tm
    row_token = jnp.zeros((p_rows,), jnp.int32).at[pos].set(jnp.arange(TOP_K * t, dtype=jnp.int32) // TOP_K)
    tile_start = jnp.arange(p_rows // tm, dtype=jnp.int32) * tm
    n_valid = (ends[-1] // tm).astype(jnp.int32).reshape(1)
    last_start = jnp.minimum(tile_start, ends[-1] - tm)
    tile_expert = jnp.minimum(jnp.searchsorted(ends, last_start, side="right"), n_experts - 1).astype(jnp.int32)
    return pos.reshape(t, TOP_K), row_token, tile_expert, n_valid


def _tile(dim, want):
    return min(dim, want)


def kernel(x, l0_w_in, l0_conv_w, l0_conv_b, l0_w_rgate, l0_b_rgate, l0_w_igate, l0_b_igate, l0_lru_lambda, l0_w_out, l0_ln1_g, l0_ln1_b, l0_w_gate_up, l0_w_down, l0_ln2_g, l0_ln2_b, l1_w_qkv, l1_b_qkv, l1_sinks, l1_w_o, l1_ln1_g, l1_ln1_b, l1_w_router, l1_b_router, l1_we_gate_up, l1_we_down, l1_ln2_g, l1_ln2_b):
    batch, seq, d = x.shape
    t = batch * seq
    depth = 2
    alpha = (2 * depth) ** 0.25
    n_q = l1_sinks.shape[0]
    n_kv = (l1_w_qkv.shape[1] // HEAD_DIM - n_q) // 2
    n_experts = l1_w_router.shape[1]

    x2d = x.reshape(t, d)
    tm_cast = _tile(d, 512)
    xb = _cast_bf16(x2d, tm_cast)
    bf = lambda w: w.astype(BF16)
    w_out_b = _cast_bf16(l0_w_out, tm_cast)
    w_down_b = _cast_bf16(l0_w_down, tm_cast)
    w_o_b = _cast_bf16(l1_w_o, tm_cast)

    tm_big = _tile(t, 1024)
    tm_ln = _tile(t, 512)

    tn_pair = _tile(d // 2, 512)
    u, gg = _pair_matmul(xb, l0_w_in, tm_big, tn_pair, _proj_epilogue, (F32, BF16), "l0_proj")
    hg = _lru(u, gg, l0_conv_w, l0_conv_b, bf(l0_w_rgate), l0_b_rgate, bf(l0_w_igate), l0_b_igate,
              l0_lru_lambda, batch, _tile(seq, 256))
    x1 = _mm_ln(hg, w_out_b, x2d, l0_ln1_g, l0_ln1_b, alpha, tm_ln, _tile(d // 2, 1024))
    (h,) = _pair_matmul(x1, l0_w_gate_up, tm_big, tn_pair, _swiglu_epilogue, (BF16,), "swiglu_up")
    x2 = _mm_ln(h, w_down_b, x1, l0_ln2_g, l0_ln2_b, alpha, tm_ln, _tile(h.shape[1] // 2, 1024))

    pos = jnp.arange(seq, dtype=F32)
    inv = ROPE_THETA ** (-jnp.arange(0, HEAD_DIM, 2, dtype=F32) / HEAD_DIM)
    ang = pos[:, None] * inv[None, :]
    cos_f = jnp.concatenate([jnp.cos(ang), jnp.cos(ang)], axis=-1)
    sin_f = jnp.concatenate([-jnp.sin(ang), jnp.sin(ang)], axis=-1)
    qkv = _qkv(x2, l1_w_qkv, l1_b_qkv, cos_f, sin_f, (n_q + n_kv) * HEAD_DIM,
               _tile(seq, 1024), _tile(n_kv * HEAD_DIM, 512))
    o = _swa(qkv, l1_sinks, n_q, n_kv, seq, _tile(seq, 512))
    tn_down = _tile(d // 2, 1024)
    x3, route = _mm_ln(o, w_o_b, x2, l1_ln1_g, l1_ln1_b, alpha, tm_ln, _tile(d // 2, 1024),
                       router=(l1_w_router, l1_b_router), pack_group=tn_down)

    tm_e = _tile(t, 512)
    pos2, row_token, tile_expert, n_valid = _route_plan(route, n_experts, tm_e)
    xs = _gather_rows(x3, row_token, tm_e, tn_down)
    hs = _moe_up(tile_expert, n_valid, xs, l1_we_gate_up, tm_e, _tile(d // 2, 512))
    ys = _moe_down(tile_expert, n_valid, hs, l1_we_down, tm_e, tn_down)
    out = _moe_combine(x3, ys, pos2, route, l1_ln2_g, l1_ln2_b, alpha, _tile(t, 256), tn_down,
                       x.dtype)
    return out.reshape(batch, seq, d)
```

```python
import functools

import jax
import jax.numpy as jnp
from jax import lax
from jax.experimental import pallas as pl
from jax.experimental.pallas import tpu as pltpu

F32 = jnp.float32
BF16 = jnp.bfloat16

HEAD_DIM = 128
WINDOW = 128
CONV_WIDTH = 4
LRU_C = 8.0
TOP_K = 2
ROPE_THETA = 10000.0
LN_EPS = 1e-5
LANES = 128
VMEM_LIMIT_BYTES = 56 * 1024 * 1024
LN_SLAB = 64
LN_CHUNK = 16


def _params(*sem):
    return pltpu.CompilerParams(dimension_semantics=sem, vmem_limit_bytes=VMEM_LIMIT_BYTES)


def _pack_pair(lo, hi):
    lo_bits = lax.bitcast_convert_type(lo.astype(BF16).astype(F32), jnp.uint32)
    hi_bits = lax.bitcast_convert_type(hi.astype(BF16).astype(F32), jnp.uint32)
    return (lo_bits >> 16) | hi_bits


def _unpack_pair(w):
    lo = lax.bitcast_convert_type(w << 16, F32)
    hi = lax.bitcast_convert_type(w & jnp.uint32(0xFFFF0000), F32)
    return lo, hi


def _pack_rows(o, group):
    half = group // 2
    return jnp.concatenate(
        [_pack_pair(o[:, g:g + half], o[:, g + half:g + group]) for g in range(0, o.shape[1], group)],
        axis=1)


def _ln_finish(y_ref, mu_ref, rstd_ref, o_ref, g_ref, b_ref, per_chunk=None, store=None):
    rows = y_ref.shape[0]
    inv_n = 1.0 / y_ref.shape[1]
    for s0 in range(0, rows, LN_SLAB):
        rs = slice(s0, s0 + LN_SLAB)
        d = y_ref[rs, :] - mu_ref[rs, :]
        var = jnp.sum(d * d, axis=-1, keepdims=True) * inv_n
        rstd_ref[rs, :] = lax.rsqrt(var + LN_EPS)

    def body(c, carry):
        r0 = pl.multiple_of(c * LN_CHUNK, LN_CHUNK)
        rr = pl.ds(r0, LN_CHUNK)
        o = (y_ref[rr, :] - mu_ref[rr, :]) * rstd_ref[rr, :] * g_ref[...] + b_ref[...]
        if store is None:
            o_ref[rr, :] = o.astype(o_ref.dtype)
        else:
            store(rr, o)
        if per_chunk is not None:
            per_chunk(r0)
        return carry

    lax.fori_loop(0, rows // LN_CHUNK, body, 0, unroll=2)


def _load_weights(first, w_refs, wb_refs):
    @pl.when(first)
    def _():
        for w_ref, wb_ref in zip(w_refs, wb_refs):
            w = w_ref[0] if len(w_ref.shape) == 3 else w_ref[...]
            wb_ref[...] = w.astype(wb_ref.dtype)


def _pair_kernel(x_ref, w_hbm, *rest, epilogue, n_out, nj, ni):
    outs = rest[:n_out]
    wb_ref, stage_ref, sem = rest[n_out:]
    j = pl.program_id(0)
    i = pl.program_id(1)
    ks = stage_ref.shape[2]
    tn = stage_ref.shape[3]
    cur = j % 2
    nxt = 1 - cur

    def slab_copy(jj, s, m, st):
        col = pl.multiple_of((jj + m * nj) * tn, tn)
        row = pl.multiple_of(s * ks, ks)
        return pltpu.make_async_copy(w_hbm.at[pl.ds(row, ks), pl.ds(col, tn)], stage_ref.at[st, m],
                                     sem.at[st, m])

    def land(jj, s, slot_w):
        st = s % 2
        for m in range(2):
            slab_copy(jj, s, m, st).wait()
            wb_ref[slot_w, m, pl.ds(pl.multiple_of(s * ks, ks), ks), :] = (
                stage_ref[st, m].astype(wb_ref.dtype))

    @pl.when(jnp.logical_and(j == 0, i == 0))
    def _():
        def body(s, c):
            for m in range(2):
                slab_copy(0, s, m, s % 2).start()
            land(0, s, 0)
            return c
        lax.fori_loop(0, ni, body, 0)

    has_next = j + 1 < nj

    @pl.when(has_next)
    def _():
        for m in range(2):
            slab_copy(j + 1, i, m, i % 2).start()

    @pl.when(jnp.logical_and(has_next, i > 0))
    def _():
        land(j + 1, i - 1, nxt)

    x = x_ref[...]
    y0 = jnp.dot(x, wb_ref[cur, 0], preferred_element_type=F32)
    y1 = jnp.dot(x, wb_ref[cur, 1], preferred_element_type=F32)
    epilogue(y0, y1, *outs)

    @pl.when(jnp.logical_and(has_next, i == ni - 1))
    def _():
        land(j + 1, ni - 1, nxt)


def _pair_matmul(xb, w, tm, tn, epilogue, out_dtypes, name):
    t, d = xb.shape
    f = w.shape[1] // 2
    nj = f // tn
    ni = t // tm
    ks = d // ni
    assert ks * ni == d and ks % 16 == 0
    kern = functools.partial(_pair_kernel, epilogue=epilogue, n_out=len(out_dtypes), nj=nj, ni=ni)
    return pl.pallas_call(
        kern,
        grid=(nj, ni),
        in_specs=[
            pl.BlockSpec((tm, d), lambda j, i: (i, 0)),
            pl.BlockSpec(memory_space=pl.ANY),
        ],
        out_specs=[pl.BlockSpec((tm, tn), lambda j, i: (i, j)) for _ in out_dtypes],
        out_shape=[jax.ShapeDtypeStruct((t, f), dt) for dt in out_dtypes],
        scratch_shapes=[
            pltpu.VMEM((2, 2, d, tn), BF16),
            pltpu.VMEM((2, 2, ks, tn), w.dtype),
            pltpu.SemaphoreType.DMA((2, 2)),
        ],
        compiler_params=_params("arbitrary", "arbitrary"),
        name=name,
    )(xb, w)


def _proj_epilogue(u, g, u_ref, gg_ref):
    u_ref[...] = u
    gg_ref[...] = jax.nn.gelu(g, approximate=True).astype(gg_ref.dtype)


def _swiglu_epilogue(g, u, h_ref):
    h_ref[...] = (jax.nn.silu(g) * u).astype(h_ref.dtype)


def _qkv_kernel(x_ref, w_ref, b_ref, cos_ref, sin_ref, o_ref, wb_ref, *, n_rope_tiles):
    _load_weights(pl.program_id(1) == 0, (w_ref,), (wb_ref,))
    acc = jnp.dot(x_ref[...], wb_ref[...], preferred_element_type=F32) + b_ref[...]
    rope = pl.program_id(0) < n_rope_tiles
    cos = jnp.where(rope, cos_ref[...], 1.0)
    sin = jnp.where(rope, sin_ref[...], 0.0)
    for c in range(acc.shape[1] // HEAD_DIM):
        cs = slice(c * HEAD_DIM, (c + 1) * HEAD_DIM)
        tt = acc[:, cs]
        o_ref[:, cs] = (tt * cos + pltpu.roll(tt, HEAD_DIM // 2, 1) * sin).astype(o_ref.dtype)


def _qkv(xb, w_qkv, b_qkv, cos_f, sin_f, n_rope_cols, tm, tn):
    t, d = xb.shape
    n = w_qkv.shape[1]
    seq = cos_f.shape[0]
    n_pos_tiles = seq // tm
    kern = functools.partial(_qkv_kernel, n_rope_tiles=n_rope_cols // tn)
    return pl.pallas_call(
        kern,
        grid=(n // tn, t // tm),
        in_specs=[
            pl.BlockSpec((tm, d), lambda j, i: (i, 0)),
            pl.BlockSpec((d, tn), lambda j, i: (0, j)),
            pl.BlockSpec((1, tn), lambda j, i: (0, j)),
            pl.BlockSpec((tm, HEAD_DIM), lambda j, i: (i % n_pos_tiles, 0)),
            pl.BlockSpec((tm, HEAD_DIM), lambda j, i: (i % n_pos_tiles, 0)),
        ],
        out_specs=pl.BlockSpec((tm, tn), lambda j, i: (i, j)),
        out_shape=jax.ShapeDtypeStruct((t, n), BF16),
        scratch_shapes=[pltpu.VMEM((d, tn), BF16)],
        compiler_params=_params("arbitrary", "arbitrary"),
        name="l1_qkv_rope",
    )(xb, w_qkv, b_qkv.reshape(1, n), cos_f, sin_f)


def _lru_kernel(u_ref, gg_ref, cw_ref, cb_ref, wr_ref, br_ref, wi_ref, bi_ref, lam_ref,
                hg_ref, ext_ref, a_ref, b_ref, h_ref, *, ts, n_blocks, block_w, scan_w):
    halo = 8
    s = pl.program_id(1)

    @pl.when(s == 0)
    def _():
        ext_ref[0:halo, :] = jnp.zeros((halo, ext_ref.shape[1]), F32)
        h_ref[...] = jnp.zeros_like(h_ref)

    ext_ref[halo:halo + ts, :] = u_ref[...]

    for hd in range(n_blocks):
        cs = slice(hd * block_w, (hd + 1) * block_w)
        uc = cb_ref[:, cs]
        for k in range(CONV_WIDTH):
            off = halo - (CONV_WIDTH - 1) + k
            uc = uc + cw_ref[k:k + 1, cs] * ext_ref[off:off + ts, cs]
        ucb = uc.astype(BF16)
        r = jax.nn.sigmoid(jnp.dot(ucb, wr_ref[hd], preferred_element_type=F32) + br_ref[:, cs])
        ig = jax.nn.sigmoid(jnp.dot(ucb, wi_ref[hd], preferred_element_type=F32) + bi_ref[:, cs])
        log_a = -LRU_C * r * jax.nn.softplus(-lam_ref[:, cs])
        a = jnp.exp(log_a)
        a_ref[:, cs] = a
        b_ref[:, cs] = jnp.sqrt(-jnp.tanh(log_a) * (a * a + 1.0)) * (ig * uc)

    ext_ref[0:halo, :] = ext_ref[ts:ts + halo, :]

    width = a_ref.shape[1]
    sub = 8
    row = lax.broadcasted_iota(jnp.int32, (sub, scan_w), 0)
    for c in range(width // scan_w):
        cs = pl.ds(c * scan_w, scan_w)

        def group(gi, h, cs=cs):
            rr = pl.ds(pl.multiple_of(gi * sub, sub), sub)
            a = a_ref[rr, cs]
            b = b_ref[rr, cs]
            for dist in (1, 2, 4):
                keep = row >= dist
                b = b + a * jnp.where(keep, pltpu.roll(b, dist, 0), 0.0)
                a = a * jnp.where(keep, pltpu.roll(a, dist, 0), 1.0)
            hh = a * h + b
            b_ref[rr, cs] = hh
            return hh[sub - 1:sub, :]

        h_ref[0:1, cs] = lax.fori_loop(0, ts // sub, group, h_ref[0:1, cs], unroll=2)

    hg_ref[...] = (b_ref[...] * gg_ref[...].astype(F32)).astype(hg_ref.dtype)


def _lru(u, gg, conv_w, conv_b, w_r, b_r, w_i, b_i, lam, batch, ts):
    t, w = u.shape
    n_blocks, block_w, _ = w_r.shape
    ns = t // batch // ts
    row = lambda b, s: (b * ns + s, 0)
    full2 = lambda b, s: (0, 0)
    full3 = lambda b, s: (0, 0, 0)
    kern = functools.partial(_lru_kernel, ts=ts, n_blocks=n_blocks, block_w=block_w,
                             scan_w=min(w, 1024))
    return pl.pallas_call(
        kern,
        grid=(batch, ns),
        in_specs=[
            pl.BlockSpec((ts, w), row),
            pl.BlockSpec((ts, w), row),
            pl.BlockSpec((CONV_WIDTH, w), full2),
            pl.BlockSpec((1, w), full2),
            pl.BlockSpec((n_blocks, block_w, block_w), full3),
            pl.BlockSpec((1, w), full2),
            pl.BlockSpec((n_blocks, block_w, block_w), full3),
            pl.BlockSpec((1, w), full2),
            pl.BlockSpec((1, w), full2),
        ],
        out_specs=pl.BlockSpec((ts, w), row),
        out_shape=jax.ShapeDtypeStruct((t, w), BF16),
        scratch_shapes=[
            pltpu.VMEM((ts + 8, w), F32),
            pltpu.VMEM((ts, w), F32),
            pltpu.VMEM((ts, w), F32),
            pltpu.VMEM((8, w), F32),
        ],
        compiler_params=_params("arbitrary", "arbitrary"),
        name="l0_rglru",
    )(u, gg, conv_w, conv_b.reshape(1, w), w_r, b_r.reshape(1, w), w_i, b_i.reshape(1, w),
      lam.reshape(1, w))


def _mm_ln_kernel(a_ref, w_ref, r_ref, g_ref, b_ref, *rest, nk, alpha, n_experts, pack_group):
    if n_experts:
        wr_ref, br_ref, o_ref, route_ref, acc_ref, mu_ref, rstd_ref, xb_ref = rest
    else:
        o_ref, acc_ref, mu_ref, rstd_ref = rest
    k = pl.program_id(1)

    @pl.when(k == 0)
    def _():
        acc_ref[...] = jnp.dot(a_ref[...], w_ref[...], preferred_element_type=F32)

    @pl.when(jnp.logical_and(k > 0, k < nk - 1))
    def _():
        acc_ref[...] += jnp.dot(a_ref[...], w_ref[...], preferred_element_type=F32)

    @pl.when(k == nk - 1)
    def _():
        y = (acc_ref[...] + jnp.dot(a_ref[...], w_ref[...], preferred_element_type=F32)
             + alpha * r_ref[...].astype(F32))
        acc_ref[...] = y
        mu_ref[...] = jnp.mean(y, axis=-1, keepdims=True)
        if n_experts:
            def store(rr, o):
                xb_ref[rr, :] = o.astype(xb_ref.dtype)
                o_ref[rr, :] = _pack_rows(o, pack_group)

            _ln_finish(acc_ref, mu_ref, rstd_ref, o_ref, g_ref, b_ref, store=store)
            logits = jnp.dot(xb_ref[...], wr_ref[...], preferred_element_type=F32) + br_ref[...]
            route_ref[...] = _top2_route(logits, n_experts)
        else:
            _ln_finish(acc_ref, mu_ref, rstd_ref, o_ref, g_ref, b_ref)


def _top2_route(logits, n_experts):
    lane = lax.broadcasted_iota(jnp.int32, logits.shape, 1)
    neg = jnp.float32(-jnp.inf)
    lg = jnp.where(lane < n_experts, logits, neg)
    m1 = jnp.max(lg, axis=-1, keepdims=True)
    i1 = jnp.min(jnp.where(lg == m1, lane, LANES), axis=-1, keepdims=True)
    lg2 = jnp.where(lane == i1, neg, lg)
    m2 = jnp.max(lg2, axis=-1, keepdims=True)
    i2 = jnp.min(jnp.where(lg2 == m2, lane, LANES), axis=-1, keepdims=True)
    e = jnp.exp(m2 - m1)
    w1 = 1.0 / (1.0 + e)
    w2 = e / (1.0 + e)
    out = jnp.where(lane == 0, i1.astype(F32), 0.0)
    out = jnp.where(lane == 1, i2.astype(F32), out)
    out = jnp.where(lane == 2, w1, out)
    out = jnp.where(lane == 3, w2, out)
    return out


def _mm_ln(a, w, resid, g, b, alpha, tm, tk, router=None, pack_group=None):
    t, kdim = a.shape
    n = w.shape[1]
    nk = kdim // tk
    assert nk >= 2
    n_experts = 0
    in_specs = [
        pl.BlockSpec((tm, tk), lambda i, k: (i, k)),
        pl.BlockSpec((tk, n), lambda i, k: (k, 0)),
        pl.BlockSpec((tm, n), lambda i, k: (i, 0)),
        pl.BlockSpec((1, n), lambda i, k: (0, 0)),
        pl.BlockSpec((1, n), lambda i, k: (0, 0)),
    ]
    args = [a, w, resid, g.reshape(1, n), b.reshape(1, n)]
    out_specs = [pl.BlockSpec((tm, n), lambda i, k: (i, 0))]
    out_shape = [jax.ShapeDtypeStruct((t, n), BF16)]
    scratch = [pltpu.VMEM((tm, n), F32), pltpu.VMEM((tm, 1), F32), pltpu.VMEM((tm, 1), F32)]
    if router is not None:
        out_specs = [pl.BlockSpec((tm, n // 2), lambda i, k: (i, 0))]
        out_shape = [jax.ShapeDtypeStruct((t, n // 2), jnp.uint32)]
        scratch.append(pltpu.VMEM((tm, n), BF16))
        w_router, b_router = router
        n_experts = w_router.shape[1]
        wr = jnp.zeros((n, LANES), BF16).at[:, :n_experts].set(w_router.astype(BF16))
        br = jnp.zeros((1, LANES), F32).at[0, :n_experts].set(b_router.astype(F32))
        in_specs += [pl.BlockSpec((n, LANES), lambda i, k: (0, 0)),
                     pl.BlockSpec((1, LANES), lambda i, k: (0, 0))]
        args += [wr, br]
        out_specs.append(pl.BlockSpec((tm, LANES), lambda i, k: (i, 0)))
        out_shape.append(jax.ShapeDtypeStruct((t, LANES), F32))
    kern = functools.partial(_mm_ln_kernel, nk=nk, alpha=alpha, n_experts=n_experts,
                             pack_group=pack_group)
    out = pl.pallas_call(
        kern,
        grid=(t // tm, nk),
        in_specs=in_specs,
        out_specs=out_specs,
        out_shape=out_shape,
        scratch_shapes=scratch,
        compiler_params=_params("parallel", "arbitrary"),
        name="mm_res_ln",
    )(*args)
    return out if router is not None else out[0]


def _swa_kernel(sink_ref, q_ref, k_ref, v_ref, kp_ref, vp_ref, o_ref, *, rows, n_kv, group,
                tiles_per_seq):
    i = pl.program_id(0)
    nqb = rows // WINDOW
    gq = group * WINDOW
    scale = HEAD_DIM ** -0.5
    neg = jnp.float32(-jnp.inf)

    qi = lax.broadcasted_iota(jnp.int32, (gq, 2 * WINDOW), 0) % WINDOW
    kj = lax.broadcasted_iota(jnp.int32, (gq, 2 * WINDOW), 1)
    delta = qi + WINDOW - kj
    band = (delta >= 0) & (delta < WINDOW)
    bias = jnp.where(band, 0.0, neg)
    seq_start = (i % tiles_per_seq) == 0
    bias_first = jnp.where(jnp.logical_and(seq_start, kj < WINDOW), neg, bias)
    grp = lax.broadcasted_iota(jnp.int32, (gq, 1), 0) // WINDOW

    for hk in range(n_kv):
        ks = slice(hk * HEAD_DIM, (hk + 1) * HEAD_DIM)
        sink = jnp.zeros((gq, 1), F32)
        for g in range(group):
            sink = jnp.where(grp == g, sink_ref[hk * group + g], sink)
        for qb in range(nqb):
            rs = slice(qb * WINDOW, (qb + 1) * WINDOW)
            if qb == 0:
                k_prev, v_prev = kp_ref[:, ks], vp_ref[:, ks]
            else:
                ps = slice((qb - 1) * WINDOW, qb * WINDOW)
                k_prev, v_prev = k_ref[ps, ks], v_ref[ps, ks]
            k_win = jnp.concatenate([k_prev, k_ref[rs, ks]], axis=0)
            v_win = jnp.concatenate([v_prev, v_ref[rs, ks]], axis=0)
            q4 = jnp.concatenate(
                [q_ref[rs, (hk * group + g) * HEAD_DIM:(hk * group + g + 1) * HEAD_DIM]
                 for g in range(group)], axis=0)
            sc = lax.dot_general(q4, k_win, (((1,), (1,)), ((), ())),
                                 preferred_element_type=F32) * scale
            sc = sc + (bias_first if qb == 0 else bias)
            m = jnp.maximum(jnp.max(sc, axis=-1, keepdims=True), sink)
            p = jnp.exp(sc - m)
            denom = jnp.sum(p, axis=-1, keepdims=True) + jnp.exp(sink - m)
            o = jnp.dot((p / denom).astype(v_win.dtype), v_win, preferred_element_type=F32)
            for g in range(group):
                hs = slice((hk * group + g) * HEAD_DIM, (hk * group + g + 1) * HEAD_DIM)
                o_ref[rs, hs] = o[g * WINDOW:(g + 1) * WINDOW, :].astype(o_ref.dtype)


def _swa(qkv, sinks, n_q, n_kv, seq, rows):
    t = qkv.shape[0]
    dq = n_q * HEAD_DIM
    dkv = n_kv * HEAD_DIM
    assert dq % dkv == 0
    k_blk = dq // dkv
    nqb = rows // WINDOW
    kern = functools.partial(_swa_kernel, rows=rows, n_kv=n_kv, group=n_q // n_kv,
                             tiles_per_seq=seq // rows)
    prev = lambda c: (lambda i, s: (jnp.maximum(i * nqb - 1, 0), c))
    grid_spec = pltpu.PrefetchScalarGridSpec(
        num_scalar_prefetch=1,
        grid=(t // rows,),
        in_specs=[
            pl.BlockSpec((rows, dq), lambda i, s: (i, 0)),
            pl.BlockSpec((rows, dkv), lambda i, s: (i, k_blk)),
            pl.BlockSpec((rows, dkv), lambda i, s: (i, k_blk + 1)),
            pl.BlockSpec((WINDOW, dkv), prev(k_blk)),
            pl.BlockSpec((WINDOW, dkv), prev(k_blk + 1)),
        ],
        out_specs=pl.BlockSpec((rows, dq), lambda i, s: (i, 0)),
    )
    return pl.pallas_call(
        kern,
        grid_spec=grid_spec,
        out_shape=jax.ShapeDtypeStruct((t, dq), BF16),
        compiler_params=_params("arbitrary"),
        name="l1_swa",
    )(sinks.astype(F32), qkv, qkv, qkv, qkv, qkv)


def _cast_kernel(x_ref, o_ref):
    o_ref[...] = x_ref[...].astype(o_ref.dtype)


def _cast_bf16(x, tm):
    r, c = x.shape
    return pl.pallas_call(
        _cast_kernel,
        grid=(r // tm,),
        in_specs=[pl.BlockSpec((tm, c), lambda i: (i, 0))],
        out_specs=pl.BlockSpec((tm, c), lambda i: (i, 0)),
        out_shape=jax.ShapeDtypeStruct((r, c), BF16),
        compiler_params=_params("parallel"),
        name="cast_bf16",
    )(x)


ROWS_PER_ISSUE = 8


def _row_copy(src_ref, idx, dst_ref, r, sem):
    return pltpu.make_async_copy(src_ref.at[pl.ds(idx, 1), :], dst_ref.at[pl.ds(r, 1), :], sem)


def _gather_rows_kernel(idx_ref, idx_next_ref, src_ref, o_ref, buf_ref, sem, *, group):
    i = pl.program_id(0)
    n = pl.num_programs(0)
    rows = o_ref.shape[0]
    slot = i % 2

    def start_tile(p_ref, s):
        def body(c, carry):
            for k in range(ROWS_PER_ISSUE):
                r = c * ROWS_PER_ISSUE + k
                _row_copy(src_ref, p_ref[0, 0, r], buf_ref.at[s], r, sem.at[s]).start(priority=k % 2)
            return carry
        lax.fori_loop(0, rows // ROWS_PER_ISSUE, body, 0)

    @pl.when(i == 0)
    def _():
        start_tile(idx_ref, 0)

    @pl.when(i + 1 < n)
    def _():
        start_tile(idx_next_ref, 1 - slot)

    def wait(r, c):
        _row_copy(src_ref, 0, buf_ref.at[slot], r, sem.at[slot]).wait()
        return c

    lax.fori_loop(0, rows, wait, 0, unroll=8)
    half = group // 2
    for g in range(0, o_ref.shape[1], group):
        lo, hi = _unpack_pair(buf_ref[slot, :, g // 2:g // 2 + half])
        o_ref[:, g:g + half] = lo.astype(o_ref.dtype)
        o_ref[:, g + half:g + group] = hi.astype(o_ref.dtype)


def _gather_rows(src, idx, tm, group):
    p = idx.shape[0]
    dw = src.shape[1]
    d = 2 * dw
    nt = p // tm
    idx3 = idx.reshape(nt, 1, tm)
    return pl.pallas_call(
        functools.partial(_gather_rows_kernel, group=group),
        grid=(nt,),
        in_specs=[
            pl.BlockSpec((1, 1, tm), lambda i: (i, 0, 0), memory_space=pltpu.SMEM),
            pl.BlockSpec((1, 1, tm), lambda i: (jnp.minimum(i + 1, nt - 1), 0, 0),
                         memory_space=pltpu.SMEM),
            pl.BlockSpec(memory_space=pl.ANY),
        ],
        out_specs=pl.BlockSpec((tm, d), lambda i: (i, 0)),
        out_shape=jax.ShapeDtypeStruct((p, d), BF16),
        scratch_shapes=[pltpu.VMEM((2, tm, dw), src.dtype), pltpu.SemaphoreType.DMA((2,))],
        compiler_params=_params("arbitrary"),
        name="moe_gather_rows",
    )(idx3, idx3, src)


def _new_expert(te_ref):
    i = pl.program_id(1)
    return jnp.logical_or(i == 0, te_ref[i] != te_ref[jnp.maximum(i - 1, 0)])


def _moe_up_kernel(te_ref, nv_ref, x_ref, wg_ref, wu_ref, h_ref, wgb_ref, wub_ref):
    _load_weights(_new_expert(te_ref), (wg_ref, wu_ref), (wgb_ref, wub_ref))
    valid = pl.program_id(1) < nv_ref[0]

    @pl.when(valid)
    def _():
        x = x_ref[...]
        g = jnp.dot(x, wgb_ref[...], preferred_element_type=F32)
        u = jnp.dot(x, wub_ref[...], preferred_element_type=F32)
        h_ref[...] = (jax.nn.silu(g) * u).astype(h_ref.dtype)

    @pl.when(jnp.logical_not(valid))
    def _():
        h_ref[...] = jnp.zeros_like(h_ref)


def _moe_up(tile_expert, n_valid, xs, we_gate_up, tm, tn):
    p, d = xs.shape
    f = we_gate_up.shape[2] // 2
    nj = f // tn
    row = lambda j, i, te, nv: (jnp.minimum(i, nv[0] - 1), 0)
    grid_spec = pltpu.PrefetchScalarGridSpec(
        num_scalar_prefetch=2,
        grid=(nj, p // tm),
        in_specs=[
            pl.BlockSpec((tm, d), row),
            pl.BlockSpec((1, d, tn), lambda j, i, te, nv: (te[i], 0, j)),
            pl.BlockSpec((1, d, tn), lambda j, i, te, nv: (te[i], 0, j + nj)),
        ],
        out_specs=pl.BlockSpec((tm, tn), lambda j, i, te, nv: (i, j)),
        scratch_shapes=[pltpu.VMEM((d, tn), BF16), pltpu.VMEM((d, tn), BF16)],
    )
    return pl.pallas_call(
        _moe_up_kernel,
        grid_spec=grid_spec,
        out_shape=jax.ShapeDtypeStruct((p, f), BF16),
        compiler_params=_params("arbitrary", "arbitrary"),
        name="moe_up",
    )(tile_expert, n_valid, xs, we_gate_up, we_gate_up)


def _moe_down_kernel(te_ref, nv_ref, h_ref, w_ref, y_ref, wb_ref):
    _load_weights(_new_expert(te_ref), (w_ref,), (wb_ref,))
    valid = pl.program_id(1) < nv_ref[0]

    @pl.when(valid)
    def _():
        y = jnp.dot(h_ref[...], wb_ref[...], preferred_element_type=F32)
        y_ref[...] = _pack_rows(y, y.shape[1])

    @pl.when(jnp.logical_not(valid))
    def _():
        y_ref[...] = jnp.zeros_like(y_ref)


def _moe_down(tile_expert, n_valid, hs, we_down, tm, tn):
    p, f = hs.shape
    d = we_down.shape[2]
    grid_spec = pltpu.PrefetchScalarGridSpec(
        num_scalar_prefetch=2,
        grid=(d // tn, p // tm),
        in_specs=[
            pl.BlockSpec((tm, f), lambda j, i, te, nv: (jnp.minimum(i, nv[0] - 1), 0)),
            pl.BlockSpec((1, f, tn), lambda j, i, te, nv: (te[i], 0, j)),
        ],
        out_specs=pl.BlockSpec((tm, tn // 2), lambda j, i, te, nv: (i, j)),
        scratch_shapes=[pltpu.VMEM((f, tn), BF16)],
    )
    return pl.pallas_call(
        _moe_down_kernel,
        grid_spec=grid_spec,
        out_shape=jax.ShapeDtypeStruct((p, d // 2), jnp.uint32),
        compiler_params=_params("arbitrary", "arbitrary"),
        name="moe_down",
    )(tile_expert, n_valid, hs, we_down)


def _moe_combine_kernel(pos_ref, pos_next_ref, x_ref, ys_ref, route_ref, g_ref, b_ref, o_ref,
                        ybuf_ref, acc_ref, mu_ref, rstd_ref, sem, *, alpha, group):
    i = pl.program_id(0)
    n = pl.num_programs(0)
    rows = acc_ref.shape[0]
    slot = i % 2

    def start_rows(p_ref, s, r0, count):
        for r in range(count):
            for k in range(TOP_K):
                _row_copy(ys_ref, p_ref[0, 0, TOP_K * (r0 + r) + k], ybuf_ref.at[s, k], r0 + r,
                          sem.at[s]).start(priority=k % 2)

    def wait_tile(s):
        def body(r, c):
            for k in range(TOP_K):
                _row_copy(ys_ref, 0, ybuf_ref.at[s, k], r, sem.at[s]).wait()
            return c
        lax.fori_loop(0, rows, body, 0, unroll=4)

    @pl.when(i == 0)
    def _():
        lax.fori_loop(0, rows // LN_CHUNK,
                      lambda c, carry: (start_rows(pos_ref, 0, c * LN_CHUNK, LN_CHUNK), carry)[1], 0)

    wait_tile(slot)

    half = group // 2
    d = acc_ref.shape[1]
    for s0 in range(0, rows, LN_SLAB):
        rs = slice(s0, s0 + LN_SLAB)
        w1 = route_ref[rs, 2:3]
        w2 = route_ref[rs, 3:4]
        total = jnp.zeros((LN_SLAB, 1), F32)
        for g in range(0, d, group):
            ws = slice(g // 2, g // 2 + half)
            parts = zip(_unpack_pair(x_ref[rs, ws]), _unpack_pair(ybuf_ref[slot, 0, rs, ws]),
                        _unpack_pair(ybuf_ref[slot, 1, rs, ws]))
            for c0, (xv, y1, y2) in zip((g, g + half), parts):
                y = alpha * xv + w1 * y1 + w2 * y2
                acc_ref[rs, c0:c0 + half] = y
                total = total + jnp.sum(y, axis=-1, keepdims=True)
        mu_ref[rs, :] = total * (1.0 / d)

    _ln_finish(acc_ref, mu_ref, rstd_ref, o_ref, g_ref, b_ref,
               per_chunk=lambda r0: start_rows(pos_next_ref, 1 - slot, r0, LN_CHUNK))

    @pl.when(i == n - 1)
    def _():
        wait_tile(1 - slot)


def _moe_combine(x, ys, pos, route, g, b, alpha, tm, group, out_dtype):
    t, dw = x.shape
    d = 2 * dw
    nt = t // tm
    row = lambda i: (i, 0)
    pos3 = pos.reshape(nt, 1, TOP_K * tm)
    kern = functools.partial(_moe_combine_kernel, alpha=alpha, group=group)
    return pl.pallas_call(
        kern,
        grid=(nt,),
        in_specs=[
            pl.BlockSpec((1, 1, TOP_K * tm), lambda i: (i, 0, 0), memory_space=pltpu.SMEM),
            pl.BlockSpec((1, 1, TOP_K * tm), lambda i: (jnp.minimum(i + 1, nt - 1), 0, 0),
                         memory_space=pltpu.SMEM),
            pl.BlockSpec((tm, dw), row),
            pl.BlockSpec(memory_space=pl.ANY),
            pl.BlockSpec((tm, LANES), row),
            pl.BlockSpec((1, d), lambda i: (0, 0)),
            pl.BlockSpec((1, d), lambda i: (0, 0)),
        ],
        out_specs=pl.BlockSpec((tm, d), row),
        out_shape=jax.ShapeDtypeStruct((t, d), out_dtype),
        scratch_shapes=[
            pltpu.VMEM((2, TOP_K, tm, dw), ys.dtype),
            pltpu.VMEM((tm, d), F32),
            pltpu.VMEM((tm, 1), F32),
            pltpu.VMEM((tm, 1), F32),
            pltpu.SemaphoreType.DMA((2,)),
        ],
        compiler_params=_params("arbitrary"),
        name="moe_combine_ln",
    )(pos3, pos3, x, ys, route, g.reshape(1, d), b.reshape(1, d))


def _route_plan(route, n_experts, tm):
    t = route.shape[0]
    e = route[:, :TOP_K].astype(jnp.int32).reshape(-1)
    onehot = (e[:, None] == jnp.arange(n_experts, dtype=jnp.int32)[None, :]).astype(jnp.int32)
    cum = jnp.cumsum(onehot, axis=0)
    rank = jnp.take_along_axis(cum, e[:, None], axis=1)[:, 0] - 1
    counts = cum[-1]
    padded = ((counts + tm - 1) // tm) * tm
    ends = jnp.cumsum(padded)
    pos = (ends - padded)[e] + rank
    p_rows = TOP_K * t + n_experts * tm
    row_token = jnp.zeros((p_rows,), jnp.int32).at[pos].set(jnp.arange(TOP_K * t, dtype=jnp.int32) // TOP_K)
    tile_start = jnp.arange(p_rows // tm, dtype=jnp.int32) * tm
    n_valid = (ends[-1] // tm).astype(jnp.int32).reshape(1)
    last_start = jnp.minimum(tile_start, ends[-1] - tm)
    tile_expert = jnp.minimum(jnp.searchsorted(ends, last_start, side="right"), n_experts - 1).astype(jnp.int32)
    return pos.reshape(t, TOP_K), row_token, tile_expert, n_valid


def _tile(dim, want):
    return min(dim, want)


def kernel(x, l0_w_in, l0_conv_w, l0_conv_b, l0_w_rgate, l0_b_rgate, l0_w_igate, l0_b_igate, l0_lru_lambda, l0_w_out, l0_ln1_g, l0_ln1_b, l0_w_gate_up, l0_w_down, l0_ln2_g, l0_ln2_b, l1_w_qkv, l1_b_qkv, l1_sinks, l1_w_o, l1_ln1_g, l1_ln1_b, l1_w_router, l1_b_router, l1_we_gate_up, l1_we_down, l1_ln2_g, l1_ln2_b):
    batch, seq, d = x.shape
    t = batch * seq
    depth = 2
    alpha = (2 * depth) ** 0.25
    n_q = l1_sinks.shape[0]
    n_kv = (l1_w_qkv.shape[1] // HEAD_DIM - n_q) // 2
    n_experts = l1_w_router.shape[1]

    x2d = x.reshape(t, d)
    tm_cast = _tile(d, 512)
    xb = _cast_bf16(x2d, tm_cast)
    bf = lambda w: w.astype(BF16)
    w_out_b = _cast_bf16(l0_w_out, tm_cast)
    w_down_b = _cast_bf16(l0_w_down, tm_cast)
    w_o_b = _cast_bf16(l1_w_o, tm_cast)

    tm_big = _tile(t, 1024)
    tm_ln = _tile(t, 512)

    tn_pair = _tile(d // 2, 512)
    u, gg = _pair_matmul(xb, l0_w_in, tm_big, tn_pair, _proj_epilogue, (F32, BF16), "l0_proj")
    hg = _lru(u, gg, l0_conv_w, l0_conv_b, bf(l0_w_rgate), l0_b_rgate, bf(l0_w_igate), l0_b_igate,
              l0_lru_lambda, batch, _tile(seq, 256))
    x1 = _mm_ln(hg, w_out_b, x2d, l0_ln1_g, l0_ln1_b, alpha, tm_ln, _tile(d // 2, 1024))
    (h,) = _pair_matmul(x1, l0_w_gate_up, tm_big, tn_pair, _swiglu_epilogue, (BF16,), "swiglu_up")
    x2 = _mm_ln(h, w_down_b, x1, l0_ln2_g, l0_ln2_b, alpha, tm_ln, _tile(h.shape[1] // 2, 1024))

    pos = jnp.arange(seq, dtype=F32)
    inv = ROPE_THETA ** (-jnp.arange(0, HEAD_DIM, 2, dtype=F32) / HEAD_DIM)
    ang = pos[:, None] * inv[None, :]
    cos_f = jnp.concatenate([jnp.cos(ang), jnp.cos(ang)], axis=-1)
    sin_f = jnp.concatenate([-jnp.sin(ang), jnp.sin(ang)], axis=-1)
    qkv = _qkv(x2, l1_w_qkv, l1_b_qkv, cos_f, sin_f, (n_q + n_kv) * HEAD_DIM,
               _tile(seq, 1024), _tile(n_kv * HEAD_DIM, 512))
    o = _swa(qkv, l1_sinks, n_q, n_kv, seq, _tile(seq, 512))
    tn_down = _tile(d // 2, 1024)
    x3, route = _mm_ln(o, w_o_b, x2, l1_ln1_g, l1_ln1_b, alpha, tm_ln, _tile(d // 2, 1024),
                       router=(l1_w_router, l1_b_router), pack_group=tn_down)

    tm_e = _tile(t, 512)
    pos2, row_token, tile_expert, n_valid = _route_plan(route, n_experts, tm_e)
    xs = _gather_rows(x3, row_token, tm_e, tn_down)
    hs = _moe_up(tile_expert, n_valid, xs, l1_we_gate_up, tm_e, _tile(d // 2, 512))
    ys = _moe_down(tile_expert, n_valid, hs, l1_we_down, tm_e, tn_down)
    out = _moe_combine(x3, ys, pos2, route, l1_ln2_g, l1_ln2_b, alpha, _tile(t, 256), tn_down,
                       x.dtype)
    return out.reshape(batch, seq, d)
```

```python
import functools

import jax
import jax.numpy as jnp
from jax import lax
from jax.experimental import pallas as pl
from jax.experimental.pallas import tpu as pltpu

F32 = jnp.float32
BF16 = jnp.bfloat16

HEAD_DIM = 128
WINDOW = 128
CONV_WIDTH = 4
LRU_C = 8.0
TOP_K = 2
ROPE_THETA = 10000.0
LN_EPS = 1e-5
LANES = 128
VMEM_LIMIT_BYTES = 56 * 1024 * 1024
LN_SLAB = 64
LN_CHUNK = 16


def _params(*sem):
    return pltpu.CompilerParams(dimension_semantics=sem, vmem_limit_bytes=VMEM_LIMIT_BYTES)


def _pack_pair(lo, hi):
    lo_bits = lax.bitcast_convert_type(lo.astype(BF16).astype(F32), jnp.uint32)
    hi_bits = lax.bitcast_convert_type(hi.astype(BF16).astype(F32), jnp.uint32)
    return (lo_bits >> 16) | hi_bits


def _unpack_pair(w):
    lo = lax.bitcast_convert_type(w << 16, F32)
    hi = lax.bitcast_convert_type(w & jnp.uint32(0xFFFF0000), F32)
    return lo, hi


def _pack_rows(o, group):
    half = group // 2
    return jnp.concatenate(
        [_pack_pair(o[:, g:g + half], o[:, g + half:g + group]) for g in range(0, o.shape[1], group)],
        axis=1)


def _ln_finish(y_ref, mu_ref, rstd_ref, o_ref, g_ref, b_ref, per_chunk=None, store=None):
    rows = y_ref.shape[0]
    inv_n = 1.0 / y_ref.shape[1]
    for s0 in range(0, rows, LN_SLAB):
        rs = slice(s0, s0 + LN_SLAB)
        d = y_ref[rs, :] - mu_ref[rs, :]
        var = jnp.sum(d * d, axis=-1, keepdims=True) * inv_n
        rstd_ref[rs, :] = lax.rsqrt(var + LN_EPS)

    def body(c, carry):
        r0 = pl.multiple_of(c * LN_CHUNK, LN_CHUNK)
        rr = pl.ds(r0, LN_CHUNK)
        o = (y_ref[rr, :] - mu_ref[rr, :]) * rstd_ref[rr, :] * g_ref[...] + b_ref[...]
        if store is None:
            o_ref[rr, :] = o.astype(o_ref.dtype)
        else:
            store(rr, o)
        if per_chunk is not None:
            per_chunk(r0)
        return carry

    lax.fori_loop(0, rows // LN_CHUNK, body, 0, unroll=2)


def _load_weights(first, w_refs, wb_refs):
    @pl.when(first)
    def _():
        for w_ref, wb_ref in zip(w_refs, wb_refs):
            w = w_ref[0] if len(w_ref.shape) == 3 else w_ref[...]
            wb_ref[...] = w.astype(wb_ref.dtype)


def _pair_kernel(x_ref, w_hbm, *rest, epilogue, n_out, n_extra, n_mats, nj, ni):
    extras = rest[:n_extra]
    outs = rest[n_extra:n_extra + n_out]
    wb_ref, stage_ref, sem = rest[n_extra + n_out:]
    j = pl.program_id(0)
    i = pl.program_id(1)
    ks = stage_ref.shape[2]
    tn = stage_ref.shape[3]
    cur = j % 2
    nxt = 1 - cur

    def slab_copy(jj, s, m, st):
        col = pl.multiple_of((jj + m * nj) * tn, tn)
        row = pl.multiple_of(s * ks, ks)
        return pltpu.make_async_copy(w_hbm.at[pl.ds(row, ks), pl.ds(col, tn)], stage_ref.at[st, m],
                                     sem.at[st, m])

    def land(jj, s, slot_w):
        st = s % 2
        for m in range(n_mats):
            slab_copy(jj, s, m, st).wait()
            wb_ref[slot_w, m, pl.ds(pl.multiple_of(s * ks, ks), ks), :] = (
                stage_ref[st, m].astype(wb_ref.dtype))

    @pl.when(jnp.logical_and(j == 0, i == 0))
    def _():
        def body(s, c):
            for m in range(n_mats):
                slab_copy(0, s, m, s % 2).start()
            land(0, s, 0)
            return c
        lax.fori_loop(0, ni, body, 0)

    has_next = j + 1 < nj

    @pl.when(has_next)
    def _():
        for m in range(n_mats):
            slab_copy(j + 1, i, m, i % 2).start()

    @pl.when(jnp.logical_and(has_next, i > 0))
    def _():
        land(j + 1, i - 1, nxt)

    x = x_ref[...]
    ys = [jnp.dot(x, wb_ref[cur, m], preferred_element_type=F32) for m in range(n_mats)]
    epilogue(*ys, *extras, *outs)

    @pl.when(jnp.logical_and(has_next, i == ni - 1))
    def _():
        land(j + 1, ni - 1, nxt)


def _pair_matmul(xb, w, tm, tn, epilogue, out_dtypes, name, n_mats=2, extras=(), extra_specs=()):
    t, d = xb.shape
    f = w.shape[1] // n_mats
    nj = f // tn
    ni = t // tm
    ks = d // ni
    assert ks * ni == d and ks % 16 == 0
    kern = functools.partial(_pair_kernel, epilogue=epilogue, n_out=len(out_dtypes),
                             n_extra=len(extras), n_mats=n_mats, nj=nj, ni=ni)
    return pl.pallas_call(
        kern,
        grid=(nj, ni),
        in_specs=[
            pl.BlockSpec((tm, d), lambda j, i: (i, 0)),
            pl.BlockSpec(memory_space=pl.ANY),
            *extra_specs,
        ],
        out_specs=[pl.BlockSpec((tm, tn), lambda j, i: (i, j)) for _ in out_dtypes],
        out_shape=[jax.ShapeDtypeStruct((t, f), dt) for dt in out_dtypes],
        scratch_shapes=[
            pltpu.VMEM((2, n_mats, d, tn), BF16),
            pltpu.VMEM((2, n_mats, ks, tn), w.dtype),
            pltpu.SemaphoreType.DMA((2, n_mats)),
        ],
        compiler_params=_params("arbitrary", "arbitrary"),
        name=name,
    )(xb, w, *extras)


def _proj_epilogue(u, g, u_ref, gg_ref):
    u_ref[...] = u
    gg_ref[...] = jax.nn.gelu(g, approximate=True).astype(gg_ref.dtype)


def _swiglu_epilogue(g, u, h_ref):
    h_ref[...] = (jax.nn.silu(g) * u).astype(h_ref.dtype)


def _qkv_epilogue(acc, b_ref, cos_ref, sin_ref, o_ref, *, n_rope_tiles):
    acc = acc + b_ref[...]
    rope = pl.program_id(0) < n_rope_tiles
    cos = jnp.where(rope, cos_ref[...], 1.0)
    sin = jnp.where(rope, sin_ref[...], 0.0)
    for c in range(acc.shape[1] // HEAD_DIM):
        cs = slice(c * HEAD_DIM, (c + 1) * HEAD_DIM)
        tt = acc[:, cs]
        o_ref[:, cs] = (tt * cos + pltpu.roll(tt, HEAD_DIM // 2, 1) * sin).astype(o_ref.dtype)


def _qkv(xb, w_qkv, b_qkv, cos_f, sin_f, n_rope_cols, tm, tn):
    n = w_qkv.shape[1]
    n_pos_tiles = cos_f.shape[0] // tm
    pos = pl.BlockSpec((tm, HEAD_DIM), lambda j, i: (i % n_pos_tiles, 0))
    (qkv,) = _pair_matmul(
        xb, w_qkv, tm, tn, functools.partial(_qkv_epilogue, n_rope_tiles=n_rope_cols // tn), (BF16,),
        "l1_qkv_rope", n_mats=1, extras=(b_qkv.reshape(1, n), cos_f, sin_f),
        extra_specs=(pl.BlockSpec((1, tn), lambda j, i: (0, j)), pos, pos))
    return qkv


def _lru_kernel(u_ref, gg_ref, cw_ref, cb_ref, wr_ref, br_ref, wi_ref, bi_ref, lam_ref,
                hg_ref, ext_ref, a_ref, b_ref, h_ref, *, ts, n_blocks, block_w, scan_w):
    halo = 8
    s = pl.program_id(1)

    @pl.when(s == 0)
    def _():
        ext_ref[0:halo, :] = jnp.zeros((halo, ext_ref.shape[1]), F32)
        h_ref[...] = jnp.zeros_like(h_ref)

    ext_ref[halo:halo + ts, :] = u_ref[...]

    for hd in range(n_blocks):
        cs = slice(hd * block_w, (hd + 1) * block_w)
        uc = cb_ref[:, cs]
        for k in range(CONV_WIDTH):
            off = halo - (CONV_WIDTH - 1) + k
            uc = uc + cw_ref[k:k + 1, cs] * ext_ref[off:off + ts, cs]
        ucb = uc.astype(BF16)
        r = jax.nn.sigmoid(jnp.dot(ucb, wr_ref[hd], preferred_element_type=F32) + br_ref[:, cs])
        ig = jax.nn.sigmoid(jnp.dot(ucb, wi_ref[hd], preferred_element_type=F32) + bi_ref[:, cs])
        log_a = -LRU_C * r * jax.nn.softplus(-lam_ref[:, cs])
        a = jnp.exp(log_a)
        a_ref[:, cs] = a
        b_ref[:, cs] = jnp.sqrt(-jnp.tanh(log_a) * (a * a + 1.0)) * (ig * uc)

    ext_ref[0:halo, :] = ext_ref[ts:ts + halo, :]

    width = a_ref.shape[1]
    sub = 8
    row = lax.broadcasted_iota(jnp.int32, (sub, scan_w), 0)
    for c in range(width // scan_w):
        cs = pl.ds(c * scan_w, scan_w)

        def group(gi, h, cs=cs):
            rr = pl.ds(pl.multiple_of(gi * sub, sub), sub)
            a = a_ref[rr, cs]
            b = b_ref[rr, cs]
            for dist in (1, 2, 4):
                keep = row >= dist
                b = b + a * jnp.where(keep, pltpu.roll(b, dist, 0), 0.0)
                a = a * jnp.where(keep, pltpu.roll(a, dist, 0), 1.0)
            hh = a * h + b
            b_ref[rr, cs] = hh
            return hh[sub - 1:sub, :]

        h_ref[0:1, cs] = lax.fori_loop(0, ts // sub, group, h_ref[0:1, cs], unroll=2)

    hg_ref[...] = (b_ref[...] * gg_ref[...].astype(F32)).astype(hg_ref.dtype)


def _lru(u, gg, conv_w, conv_b, w_r, b_r, w_i, b_i, lam, batch, ts):
    t, w = u.shape
    n_blocks, block_w, _ = w_r.shape
    ns = t // batch // ts
    row = lambda b, s: (b * ns + s, 0)
    full2 = lambda b, s: (0, 0)
    full3 = lambda b, s: (0, 0, 0)
    kern = functools.partial(_lru_kernel, ts=ts, n_blocks=n_blocks, block_w=block_w,
                             scan_w=min(w, 1024))
    return pl.pallas_call(
        kern,
        grid=(batch, ns),
        in_specs=[
            pl.BlockSpec((ts, w), row),
            pl.BlockSpec((ts, w), row),
            pl.BlockSpec((CONV_WIDTH, w), full2),
            pl.BlockSpec((1, w), full2),
            pl.BlockSpec((n_blocks, block_w, block_w), full3),
            pl.BlockSpec((1, w), full2),
            pl.BlockSpec((n_blocks, block_w, block_w), full3),
            pl.BlockSpec((1, w), full2),
            pl.BlockSpec((1, w), full2),
        ],
        out_specs=pl.BlockSpec((ts, w), row),
        out_shape=jax.ShapeDtypeStruct((t, w), BF16),
        scratch_shapes=[
            pltpu.VMEM((ts + 8, w), F32),
            pltpu.VMEM((ts, w), F32),
            pltpu.VMEM((ts, w), F32),
            pltpu.VMEM((8, w), F32),
        ],
        compiler_params=_params("arbitrary", "arbitrary"),
        name="l0_rglru",
    )(u, gg, conv_w, conv_b.reshape(1, w), w_r, b_r.reshape(1, w), w_i, b_i.reshape(1, w),
      lam.reshape(1, w))


def _mm_ln_kernel(a_ref, w_ref, r_ref, g_ref, b_ref, *rest, nk, alpha, n_experts, pack_group):
    if n_experts:
        wr_ref, br_ref, o_ref, route_ref, acc_ref, mu_ref, rstd_ref, xb_ref = rest
    else:
        o_ref, acc_ref, mu_ref, rstd_ref = rest
    k = pl.program_id(1)

    @pl.when(k == 0)
    def _():
        acc_ref[...] = jnp.dot(a_ref[...], w_ref[...], preferred_element_type=F32)

    @pl.when(jnp.logical_and(k > 0, k < nk - 1))
    def _():
        acc_ref[...] += jnp.dot(a_ref[...], w_ref[...], preferred_element_type=F32)

    @pl.when(k == nk - 1)
    def _():
        y = (acc_ref[...] + jnp.dot(a_ref[...], w_ref[...], preferred_element_type=F32)
             + alpha * r_ref[...].astype(F32))
        acc_ref[...] = y
        mu_ref[...] = jnp.mean(y, axis=-1, keepdims=True)
        if n_experts:
            def store(rr, o):
                xb_ref[rr, :] = o.astype(xb_ref.dtype)
                o_ref[rr, :] = _pack_rows(o, pack_group)

            _ln_finish(acc_ref, mu_ref, rstd_ref, o_ref, g_ref, b_ref, store=store)
            logits = jnp.dot(xb_ref[...], wr_ref[...], preferred_element_type=F32) + br_ref[...]
            route_ref[...] = _top2_route(logits, n_experts)
        else:
            _ln_finish(acc_ref, mu_ref, rstd_ref, o_ref, g_ref, b_ref)


def _top2_route(logits, n_experts):
    lane = lax.broadcasted_iota(jnp.int32, logits.shape, 1)
    neg = jnp.float32(-jnp.inf)
    lg = jnp.where(lane < n_experts, logits, neg)
    m1 = jnp.max(lg, axis=-1, keepdims=True)
    i1 = jnp.min(jnp.where(lg == m1, lane, LANES), axis=-1, keepdims=True)
    lg2 = jnp.where(lane == i1, neg, lg)
    m2 = jnp.max(lg2, axis=-1, keepdims=True)
    i2 = jnp.min(jnp.where(lg2 == m2, lane, LANES), axis=-1, keepdims=True)
    e = jnp.exp(m2 - m1)
    w1 = 1.0 / (1.0 + e)
    w2 = e / (1.0 + e)
    out = jnp.where(lane == 0, i1.astype(F32), 0.0)
    out = jnp.where(lane == 1, i2.astype(F32), out)
    out = jnp.where(lane == 2, w1, out)
    out = jnp.where(lane == 3, w2, out)
    return out


def _mm_ln(a, w, resid, g, b, alpha, tm, tk, router=None, pack_group=None):
    t, kdim = a.shape
    n = w.shape[1]
    nk = kdim // tk
    assert nk >= 2
    n_experts = 0
    in_specs = [
        pl.BlockSpec((tm, tk), lambda i, k: (i, k)),
        pl.BlockSpec((tk, n), lambda i, k: (k, 0)),
        pl.BlockSpec((tm, n), lambda i, k: (i, 0)),
        pl.BlockSpec((1, n), lambda i, k: (0, 0)),
        pl.BlockSpec((1, n), lambda i, k: (0, 0)),
    ]
    args = [a, w, resid, g.reshape(1, n), b.reshape(1, n)]
    out_specs = [pl.BlockSpec((tm, n), lambda i, k: (i, 0))]
    out_shape = [jax.ShapeDtypeStruct((t, n), BF16)]
    scratch = [pltpu.VMEM((tm, n), F32), pltpu.VMEM((tm, 1), F32), pltpu.VMEM((tm, 1), F32)]
    if router is not None:
        out_specs = [pl.BlockSpec((tm, n // 2), lambda i, k: (i, 0))]
        out_shape = [jax.ShapeDtypeStruct((t, n // 2), jnp.uint32)]
        scratch.append(pltpu.VMEM((tm, n), BF16))
        w_router, b_router = router
        n_experts = w_router.shape[1]
        wr = jnp.zeros((n, LANES), BF16).at[:, :n_experts].set(w_router.astype(BF16))
        br = jnp.zeros((1, LANES), F32).at[0, :n_experts].set(b_router.astype(F32))
        in_specs += [pl.BlockSpec((n, LANES), lambda i, k: (0, 0)),
                     pl.BlockSpec((1, LANES), lambda i, k: (0, 0))]
        args += [wr, br]
        out_specs.append(pl.BlockSpec((tm, LANES), lambda i, k: (i, 0)))
        out_shape.append(jax.ShapeDtypeStruct((t, LANES), F32))
    kern = functools.partial(_mm_ln_kernel, nk=nk, alpha=alpha, n_experts=n_experts,
                             pack_group=pack_group)
    out = pl.pallas_call(
        kern,
        grid=(t // tm, nk),
        in_specs=in_specs,
        out_specs=out_specs,
        out_shape=out_shape,
        scratch_shapes=scratch,
        compiler_params=_params("parallel", "arbitrary"),
        name="mm_res_ln",
    )(*args)
    return out if router is not None else out[0]


def _swa_kernel(sink_ref, q_ref, k_ref, v_ref, kp_ref, vp_ref, o_ref, *, rows, n_kv, group,
                tiles_per_seq):
    i = pl.program_id(0)
    nqb = rows // WINDOW
    gq = group * WINDOW
    scale = HEAD_DIM ** -0.5
    neg = jnp.float32(-jnp.inf)

    qi = lax.broadcasted_iota(jnp.int32, (gq, 2 * WINDOW), 0) % WINDOW
    kj = lax.broadcasted_iota(jnp.int32, (gq, 2 * WINDOW), 1)
    delta = qi + WINDOW - kj
    band = (delta >= 0) & (delta < WINDOW)
    bias = jnp.where(band, 0.0, neg)
    seq_start = (i % tiles_per_seq) == 0
    bias_first = jnp.where(jnp.logical_and(seq_start, kj < WINDOW), neg, bias)
    grp = lax.broadcasted_iota(jnp.int32, (gq, 1), 0) // WINDOW

    for hk in range(n_kv):
        ks = slice(hk * HEAD_DIM, (hk + 1) * HEAD_DIM)
        sink = jnp.zeros((gq, 1), F32)
        for g in range(group):
            sink = jnp.where(grp == g, sink_ref[hk * group + g], sink)
        for qb in range(nqb):
            rs = slice(qb * WINDOW, (qb + 1) * WINDOW)
            if qb == 0:
                k_prev, v_prev = kp_ref[:, ks], vp_ref[:, ks]
            else:
                ps = slice((qb - 1) * WINDOW, qb * WINDOW)
                k_prev, v_prev = k_ref[ps, ks], v_ref[ps, ks]
            k_win = jnp.concatenate([k_prev, k_ref[rs, ks]], axis=0)
            v_win = jnp.concatenate([v_prev, v_ref[rs, ks]], axis=0)
            q4 = jnp.concatenate(
                [q_ref[rs, (hk * group + g) * HEAD_DIM:(hk * group + g + 1) * HEAD_DIM]
                 for g in range(group)], axis=0)
            sc = lax.dot_general(q4, k_win, (((1,), (1,)), ((), ())),
                                 preferred_element_type=F32) * scale
            sc = sc + (bias_first if qb == 0 else bias)
            m = jnp.maximum(jnp.max(sc, axis=-1, keepdims=True), sink)
            p = jnp.exp(sc - m)
            denom = jnp.sum(p, axis=-1, keepdims=True) + jnp.exp(sink - m)
            o = jnp.dot((p / denom).astype(v_win.dtype), v_win, preferred_element_type=F32)
            for g in range(group):
                hs = slice((hk * group + g) * HEAD_DIM, (hk * group + g + 1) * HEAD_DIM)
                o_ref[rs, hs] = o[g * WINDOW:(g + 1) * WINDOW, :].astype(o_ref.dtype)


def _swa(qkv, sinks, n_q, n_kv, seq, rows):
    t = qkv.shape[0]
    dq = n_q * HEAD_DIM
    dkv = n_kv * HEAD_DIM
    assert dq % dkv == 0
    k_blk = dq // dkv
    nqb = rows // WINDOW
    kern = functools.partial(_swa_kernel, rows=rows, n_kv=n_kv, group=n_q // n_kv,
                             tiles_per_seq=seq // rows)
    prev = lambda c: (lambda i, s: (jnp.maximum(i * nqb - 1, 0), c))
    grid_spec = pltpu.PrefetchScalarGridSpec(
        num_scalar_prefetch=1,
        grid=(t // rows,),
        in_specs=[
            pl.BlockSpec((rows, dq), lambda i, s: (i, 0)),
            pl.BlockSpec((rows, dkv), lambda i, s: (i, k_blk)),
            pl.BlockSpec((rows, dkv), lambda i, s: (i, k_blk + 1)),
            pl.BlockSpec((WINDOW, dkv), prev(k_blk)),
            pl.BlockSpec((WINDOW, dkv), prev(k_blk + 1)),
        ],
        out_specs=pl.BlockSpec((rows, dq), lambda i, s: (i, 0)),
    )
    return pl.pallas_call(
        kern,
        grid_spec=grid_spec,
        out_shape=jax.ShapeDtypeStruct((t, dq), BF16),
        compiler_params=_params("arbitrary"),
        name="l1_swa",
    )(sinks.astype(F32), qkv, qkv, qkv, qkv, qkv)


def _cast_kernel(x_ref, o_ref):
    o_ref[...] = x_ref[...].astype(o_ref.dtype)


def _cast_bf16(x, tm):
    r, c = x.shape
    return pl.pallas_call(
        _cast_kernel,
        grid=(r // tm,),
        in_specs=[pl.BlockSpec((tm, c), lambda i: (i, 0))],
        out_specs=pl.BlockSpec((tm, c), lambda i: (i, 0)),
        out_shape=jax.ShapeDtypeStruct((r, c), BF16),
        compiler_params=_params("parallel"),
        name="cast_bf16",
    )(x)


def _row_copy(src_ref, idx, dst_ref, r, sem):
    return pltpu.make_async_copy(src_ref.at[pl.ds(idx, 1), :], dst_ref.at[pl.ds(r, 1), :], sem)


def _gather_rows_kernel(idx_ref, idx_next_ref, src_ref, o_ref, buf_ref, sem, *, group):
    i = pl.program_id(0)
    n = pl.num_programs(0)
    rows = o_ref.shape[0]
    slot = i % 2

    def start_tile(p_ref, s):
        def body(r, c):
            _row_copy(src_ref, p_ref[0, 0, r], buf_ref.at[s], r, sem.at[s]).start()
            return c
        lax.fori_loop(0, rows, body, 0, unroll=8)

    @pl.when(i == 0)
    def _():
        start_tile(idx_ref, 0)

    @pl.when(i + 1 < n)
    def _():
        start_tile(idx_next_ref, 1 - slot)

    def wait(r, c):
        _row_copy(src_ref, 0, buf_ref.at[slot], r, sem.at[slot]).wait()
        return c

    lax.fori_loop(0, rows, wait, 0, unroll=8)
    half = group // 2
    for g in range(0, o_ref.shape[1], group):
        lo, hi = _unpack_pair(buf_ref[slot, :, g // 2:g // 2 + half])
        o_ref[:, g:g + half] = lo.astype(o_ref.dtype)
        o_ref[:, g + half:g + group] = hi.astype(o_ref.dtype)


def _gather_rows(src, idx, tm, group):
    p = idx.shape[0]
    dw = src.shape[1]
    d = 2 * dw
    nt = p // tm
    idx3 = idx.reshape(nt, 1, tm)
    return pl.pallas_call(
        functools.partial(_gather_rows_kernel, group=group),
        grid=(nt,),
        in_specs=[
            pl.BlockSpec((1, 1, tm), lambda i: (i, 0, 0), memory_space=pltpu.SMEM),
            pl.BlockSpec((1, 1, tm), lambda i: (jnp.minimum(i + 1, nt - 1), 0, 0),
                         memory_space=pltpu.SMEM),
            pl.BlockSpec(memory_space=pl.ANY),
        ],
        out_specs=pl.BlockSpec((tm, d), lambda i: (i, 0)),
        out_shape=jax.ShapeDtypeStruct((p, d), BF16),
        scratch_shapes=[pltpu.VMEM((2, tm, dw), src.dtype), pltpu.SemaphoreType.DMA((2,))],
        compiler_params=_params("arbitrary"),
        name="moe_gather_rows",
    )(idx3, idx3, src)


def _new_expert(te_ref):
    i = pl.program_id(1)
    return jnp.logical_or(i == 0, te_ref[i] != te_ref[jnp.maximum(i - 1, 0)])


def _moe_up_kernel(te_ref, nv_ref, x_ref, wg_ref, wu_ref, h_ref, wgb_ref, wub_ref):
    _load_weights(_new_expert(te_ref), (wg_ref, wu_ref), (wgb_ref, wub_ref))
    valid = pl.program_id(1) < nv_ref[0]

    @pl.when(valid)
    def _():
        x = x_ref[...]
        g = jnp.dot(x, wgb_ref[...], preferred_element_type=F32)
        u = jnp.dot(x, wub_ref[...], preferred_element_type=F32)
        h_ref[...] = (jax.nn.silu(g) * u).astype(h_ref.dtype)

    @pl.when(jnp.logical_not(valid))
    def _():
        h_ref[...] = jnp.zeros_like(h_ref)


def _moe_up(tile_expert, n_valid, xs, we_gate_up, tm, tn):
    p, d = xs.shape
    f = we_gate_up.shape[2] // 2
    nj = f // tn
    row = lambda j, i, te, nv: (jnp.minimum(i, nv[0] - 1), 0)
    grid_spec = pltpu.PrefetchScalarGridSpec(
        num_scalar_prefetch=2,
        grid=(nj, p // tm),
        in_specs=[
            pl.BlockSpec((tm, d), row),
            pl.BlockSpec((1, d, tn), lambda j, i, te, nv: (te[i], 0, j)),
            pl.BlockSpec((1, d, tn), lambda j, i, te, nv: (te[i], 0, j + nj)),
        ],
        out_specs=pl.BlockSpec((tm, tn), lambda j, i, te, nv: (i, j)),
        scratch_shapes=[pltpu.VMEM((d, tn), BF16), pltpu.VMEM((d, tn), BF16)],
    )
    return pl.pallas_call(
        _moe_up_kernel,
        grid_spec=grid_spec,
        out_shape=jax.ShapeDtypeStruct((p, f), BF16),
        compiler_params=_params("arbitrary", "arbitrary"),
        name="moe_up",
    )(tile_expert, n_valid, xs, we_gate_up, we_gate_up)


def _moe_down_kernel(te_ref, nv_ref, h_ref, w_ref, y_ref, wb_ref):
    _load_weights(_new_expert(te_ref), (w_ref,), (wb_ref,))
    valid = pl.program_id(1) < nv_ref[0]

    @pl.when(valid)
    def _():
        y = jnp.dot(h_ref[...], wb_ref[...], preferred_element_type=F32)
        y_ref[...] = _pack_rows(y, y.shape[1])

    @pl.when(jnp.logical_not(valid))
    def _():
        y_ref[...] = jnp.zeros_like(y_ref)


def _moe_down(tile_expert, n_valid, hs, we_down, tm, tn):
    p, f = hs.shape
    d = we_down.shape[2]
    grid_spec = pltpu.PrefetchScalarGridSpec(
        num_scalar_prefetch=2,
        grid=(d // tn, p // tm),
        in_specs=[
            pl.BlockSpec((tm, f), lambda j, i, te, nv: (jnp.minimum(i, nv[0] - 1), 0)),
            pl.BlockSpec((1, f, tn), lambda j, i, te, nv: (te[i], 0, j)),
        ],
        out_specs=pl.BlockSpec((tm, tn // 2), lambda j, i, te, nv: (i, j)),
        scratch_shapes=[pltpu.VMEM((f, tn), BF16)],
    )
    return pl.pallas_call(
        _moe_down_kernel,
        grid_spec=grid_spec,
        out_shape=jax.ShapeDtypeStruct((p, d // 2), jnp.uint32),
        compiler_params=_params("arbitrary", "arbitrary"),
        name="moe_down",
    )(tile_expert, n_valid, hs, we_down)


def _moe_combine_kernel(pos_ref, pos_next_ref, x_ref, ys_ref, route_ref, g_ref, b_ref, o_ref,
                        ybuf_ref, acc_ref, mu_ref, rstd_ref, sem, *, alpha, group):
    i = pl.program_id(0)
    n = pl.num_programs(0)
    rows = acc_ref.shape[0]
    slot = i % 2

    def start_rows(p_ref, s, r0, count):
        for r in range(count):
            for k in range(TOP_K):
                _row_copy(ys_ref, p_ref[0, 0, TOP_K * (r0 + r) + k], ybuf_ref.at[s, k], r0 + r,
                          sem.at[s]).start()

    def wait_tile(s):
        def body(r, c):
            for k in range(TOP_K):
                _row_copy(ys_ref, 0, ybuf_ref.at[s, k], r, sem.at[s]).wait()
            return c
        lax.fori_loop(0, rows, body, 0, unroll=4)

    @pl.when(i == 0)
    def _():
        lax.fori_loop(0, rows // LN_CHUNK,
                      lambda c, carry: (start_rows(pos_ref, 0, c * LN_CHUNK, LN_CHUNK), carry)[1], 0)

    wait_tile(slot)

    half = group // 2
    d = acc_ref.shape[1]
    for s0 in range(0, rows, LN_SLAB):
        rs = slice(s0, s0 + LN_SLAB)
        w1 = route_ref[rs, 2:3]
        w2 = route_ref[rs, 3:4]
        total = jnp.zeros((LN_SLAB, 1), F32)
        for g in range(0, d, group):
            ws = slice(g // 2, g // 2 + half)
            parts = zip(_unpack_pair(x_ref[rs, ws]), _unpack_pair(ybuf_ref[slot, 0, rs, ws]),
                        _unpack_pair(ybuf_ref[slot, 1, rs, ws]))
            for c0, (xv, y1, y2) in zip((g, g + half), parts):
                y = alpha * xv + w1 * y1 + w2 * y2
                acc_ref[rs, c0:c0 + half] = y
                total = total + jnp.sum(y, axis=-1, keepdims=True)
        mu_ref[rs, :] = total * (1.0 / d)

    _ln_finish(acc_ref, mu_ref, rstd_ref, o_ref, g_ref, b_ref,
               per_chunk=lambda r0: start_rows(pos_next_ref, 1 - slot, r0, LN_CHUNK))

    @pl.when(i == n - 1)
    def _():
        wait_tile(1 - slot)


def _moe_combine(x, ys, pos, route, g, b, alpha, tm, group, out_dtype):
    t, dw = x.shape
    d = 2 * dw
    nt = t // tm
    row = lambda i: (i, 0)
    pos3 = pos.reshape(nt, 1, TOP_K * tm)
    kern = functools.partial(_moe_combine_kernel, alpha=alpha, group=group)
    return pl.pallas_call(
        kern,
        grid=(nt,),
        in_specs=[
            pl.BlockSpec((1, 1, TOP_K * tm), lambda i: (i, 0, 0), memory_space=pltpu.SMEM),
            pl.BlockSpec((1, 1, TOP_K * tm), lambda i: (jnp.minimum(i + 1, nt - 1), 0, 0),
                         memory_space=pltpu.SMEM),
            pl.BlockSpec((tm, dw), row),
            pl.BlockSpec(memory_space=pl.ANY),
            pl.BlockSpec((tm, LANES), row),
            pl.BlockSpec((1, d), lambda i: (0, 0)),
            pl.BlockSpec((1, d), lambda i: (0, 0)),
        ],
        out_specs=pl.BlockSpec((tm, d), row),
        out_shape=jax.ShapeDtypeStruct((t, d), out_dtype),
        scratch_shapes=[
            pltpu.VMEM((2, TOP_K, tm, dw), ys.dtype),
            pltpu.VMEM((tm, d), F32),
            pltpu.VMEM((tm, 1), F32),
            pltpu.VMEM((tm, 1), F32),
            pltpu.SemaphoreType.DMA((2,)),
        ],
        compiler_params=_params("arbitrary"),
        name="moe_combine_ln",
    )(pos3, pos3, x, ys, route, g.reshape(1, d), b.reshape(1, d))


def _route_plan(route, n_experts, tm):
    t = route.shape[0]
    e = route[:, :TOP_K].astype(jnp.int32).reshape(-1)
    onehot = (e[:, None] == jnp.arange(n_experts, dtype=jnp.int32)[None, :]).astype(jnp.int32)
    cum = jnp.cumsum(onehot, axis=0)
    rank = jnp.take_along_axis(cum, e[:, None], axis=1)[:, 0] - 1
    counts = cum[-1]
    padded = ((counts + tm - 1) // tm) * tm
    ends = jnp.cumsum(padded)
    pos = (ends - padded)[e] + rank
    p_rows = TOP_K * t + n_experts * tm
    row_token = jnp.zeros((p_rows,), jnp.int32).at[pos].set(jnp.arange(TOP_K * t, dtype=jnp.int32) // TOP_K)
    tile_start = jnp.arange(p_rows // tm, dtype=jnp.int32) * tm
    n_valid = (ends[-1] // tm).astype(jnp.int32).reshape(1)
    last_start = jnp.minimum(tile_start, ends[-1] - tm)
    tile_expert = jnp.minimum(jnp.searchsorted(ends, last_start, side="right"), n_experts - 1).astype(jnp.int32)
    return pos.reshape(t, TOP_K), row_token, tile_expert, n_valid


def _tile(dim, want):
    return min(dim, want)


def kernel(x, l0_w_in, l0_conv_w, l0_conv_b, l0_w_rgate, l0_b_rgate, l0_w_igate, l0_b_igate, l0_lru_lambda, l0_w_out, l0_ln1_g, l0_ln1_b, l0_w_gate_up, l0_w_down, l0_ln2_g, l0_ln2_b, l1_w_qkv, l1_b_qkv, l1_sinks, l1_w_o, l1_ln1_g, l1_ln1_b, l1_w_router, l1_b_router, l1_we_gate_up, l1_we_down, l1_ln2_g, l1_ln2_b):
    batch, seq, d = x.shape
    t = batch * seq
    depth = 2
    alpha = (2 * depth) ** 0.25
    n_q = l1_sinks.shape[0]
    n_kv = (l1_w_qkv.shape[1] // HEAD_DIM - n_q) // 2
    n_experts = l1_w_router.shape[1]

    x2d = x.reshape(t, d)
    tm_cast = _tile(d, 512)
    xb = _cast_bf16(x2d, tm_cast)
    bf = lambda w: w.astype(BF16)
    w_out_b = _cast_bf16(l0_w_out, tm_cast)
    w_down_b = _cast_bf16(l0_w_down, tm_cast)
    w_o_b = _cast_bf16(l1_w_o, tm_cast)

    tm_big = _tile(t, 1024)
    tm_ln = _tile(t, 512)

    tn_pair = _tile(d // 2, 512)
    u, gg = _pair_matmul(xb, l0_w_in, tm_big, tn_pair, _proj_epilogue, (F32, BF16), "l0_proj")
    hg = _lru(u, gg, l0_conv_w, l0_conv_b, bf(l0_w_rgate), l0_b_rgate, bf(l0_w_igate), l0_b_igate,
              l0_lru_lambda, batch, _tile(seq, 256))
    x1 = _mm_ln(hg, w_out_b, x2d, l0_ln1_g, l0_ln1_b, alpha, tm_ln, _tile(d // 2, 1024))
    (h,) = _pair_matmul(x1, l0_w_gate_up, tm_big, tn_pair, _swiglu_epilogue, (BF16,), "swiglu_up")
    x2 = _mm_ln(h, w_down_b, x1, l0_ln2_g, l0_ln2_b, alpha, tm_ln, _tile(h.shape[1] // 2, 1024))

    pos = jnp.arange(seq, dtype=F32)
    inv = ROPE_THETA ** (-jnp.arange(0, HEAD_DIM, 2, dtype=F32) / HEAD_DIM)
    ang = pos[:, None] * inv[None, :]
    cos_f = jnp.concatenate([jnp.cos(ang), jnp.cos(ang)], axis=-1)
    sin_f = jnp.concatenate([-jnp.sin(ang), jnp.sin(ang)], axis=-1)
    qkv = _qkv(x2, l1_w_qkv, l1_b_qkv, cos_f, sin_f, (n_q + n_kv) * HEAD_DIM,
               _tile(seq, 1024), _tile(n_kv * HEAD_DIM, 1024))
    o = _swa(qkv, l1_sinks, n_q, n_kv, seq, _tile(seq, 512))
    tn_down = _tile(d // 2, 1024)
    x3, route = _mm_ln(o, w_o_b, x2, l1_ln1_g, l1_ln1_b, alpha, tm_ln, _tile(d // 2, 1024),
                       router=(l1_w_router, l1_b_router), pack_group=tn_down)

    tm_e = _tile(t, 512)
    pos2, row_token, tile_expert, n_valid = _route_plan(route, n_experts, tm_e)
    xs = _gather_rows(x3, row_token, tm_e, tn_down)
    hs = _moe_up(tile_expert, n_valid, xs, l1_we_gate_up, tm_e, _tile(d // 2, 512))
    ys = _moe_down(tile_expert, n_valid, hs, l1_we_down, tm_e, tn_down)
    out = _moe_combine(x3, ys, pos2, route, l1_ln2_g, l1_ln2_b, alpha, _tile(t, 256), tn_down,
                       x.dtype)
    return out.reshape(batch, seq, d)
```
